```python
import jax, jax.numpy as jnp
from jax import lax
import numpy as np

D_MODEL = 2048
BATCH = 4
SEQ = 8192
DEPTH = 2

GRID_W = 64
CTX_LEN = 256
MIX_WIDTH = D_MODEL
FOURIER_WIDTH = MIX_WIDTH // 2
LRU_WIDTH = MIX_WIDTH - FOURIER_WIDTH
FOURIER_GROUPS = 4
FOURIER_GROUP_DIM = FOURIER_WIDTH // FOURIER_GROUPS
LRU_HEADS = 4
LRU_HEAD_DIM = LRU_WIDTH // LRU_HEADS
CONV_WIDTH = 4
CONV_LEFT = 2
LRU_C = 8.0
FFN_MULT = 256
D_FF = ((8 * D_MODEL + 3 * FFN_MULT - 1) // (3 * FFN_MULT)) * FFN_MULT
IN_COLS = FOURIER_WIDTH + 2 * LRU_WIDTH
N_MOD = 6
N_NORMS = 4
EPS = 1e-6

kernel_name = "hybrid_fourier_rglru_dit_block"


def rms_norm(x, g):
    xf = x.astype(jnp.float32)
    y = xf * lax.rsqrt(jnp.mean(xf * xf, axis=-1, keepdims=True) + EPS)
    return (y * g.astype(jnp.float32)).astype(x.dtype)


def modulate(h, shift, scale):
    return h * (1 + scale) + shift


def fourier_grid(u, w_four):
    bsz, length, _ = u.shape
    rows = length // GRID_W
    uf = u.astype(jnp.float32).reshape(bsz, rows, GRID_W, FOURIER_GROUPS, FOURIER_GROUP_DIM)
    z = jnp.fft.fftn(uf, axes=(1, 2, 4), norm="ortho").real
    z = z.reshape(bsz, length, FOURIER_GROUPS, FOURIER_GROUP_DIM).astype(u.dtype)
    return jnp.einsum("blgi,gij->blgj", z, w_four).reshape(bsz, length, FOURIER_WIDTH)


def fourier_seq(u, w_four):
    bsz, length, _ = u.shape
    uf = u.astype(jnp.float32).reshape(bsz, length, FOURIER_GROUPS, FOURIER_GROUP_DIM)
    z = jnp.fft.fftn(uf, axes=(1, 3), norm="ortho").real.astype(u.dtype)
    return jnp.einsum("blgi,gij->blgj", z, w_four).reshape(bsz, length, FOURIER_WIDTH)


def centred_depthwise_conv(u, w_conv, b_conv):
    length = u.shape[1]
    up = jnp.pad(u, ((0, 0), (CONV_LEFT, CONV_WIDTH - 1 - CONV_LEFT), (0, 0)))
    out = b_conv
    for k in range(CONV_WIDTH):
        out = out + w_conv[k] * up[:, k:k + length]
    return out


def rglru_coeffs(u, w_gates, b_gates, lam):
    bsz, length, _ = u.shape
    uh = u.reshape(bsz, length, LRU_HEADS, LRU_HEAD_DIM)
    gates = jnp.einsum("blhi,ghij->gblhj", uh, w_gates).reshape(2, bsz, length, LRU_WIDTH)
    gates = gates.astype(jnp.float32) + b_gates[:, None, None, :].astype(jnp.float32)
    r = jax.nn.sigmoid(gates[0])
    i = jax.nn.sigmoid(gates[1])
    log_a = -LRU_C * r * jax.nn.softplus(-lam.astype(jnp.float32))
    a = jnp.exp(log_a)
    x_in = jnp.sqrt(-jnp.expm1(2.0 * log_a)) * (i * u.astype(jnp.float32))
    return a, x_in


def _affine_combine(e1, e2):
    a1, b1 = e1
    a2, b2 = e2
    return a1 * a2, a2 * b1 + b2


def linear_scan(a, bx, h0):
    bx = bx.at[:, 0].add(a[:, 0] * h0)
    _, h = lax.associative_scan(_affine_combine, (a, bx), axis=1)
    return h


def mixing_sublayer(h_lat, h_ctx, w_in, w_four, conv_w, conv_b, lru_w, lru_b, lru_lam, w_out, with_ctx_out):
    split_at = (FOURIER_WIDTH, FOURIER_WIDTH + LRU_WIDTH)
    f_lat, r_lat, g_lat = jnp.split(h_lat @ w_in, split_at, axis=-1)
    f_ctx, r_ctx, g_ctx = jnp.split(h_ctx @ w_in, split_at, axis=-1)
    r_lat = centred_depthwise_conv(r_lat, conv_w, conv_b)
    r_ctx = centred_depthwise_conv(r_ctx, conv_w, conv_b)
    rec_lat, rec_ctx = [], []
    for d in range(2):
        a_c, b_c = rglru_coeffs(r_ctx, lru_w[d], lru_b[d], lru_lam[d])
        a_l, b_l = rglru_coeffs(r_lat, lru_w[d], lru_b[d], lru_lam[d])
        if d == 1:
            a_c, b_c, a_l, b_l = (jnp.flip(t, axis=1) for t in (a_c, b_c, a_l, b_l))
        h_c = linear_scan(a_c, b_c, jnp.zeros_like(a_c[:, 0]))
        h_l = linear_scan(a_l, b_l, h_c[:, -1])
        if d == 1:
            h_c, h_l = jnp.flip(h_c, axis=1), jnp.flip(h_l, axis=1)
        rec_lat.append(h_l)
        rec_ctx.append(h_c)
    rec_l = (rec_lat[0] + rec_lat[1]).astype(h_lat.dtype)
    y_lat = jnp.concatenate([fourier_grid(f_lat, w_four), jax.nn.gelu(g_lat) * rec_l], axis=-1) @ w_out
    if not with_ctx_out:
        return y_lat, None
    rec_c = (rec_ctx[0] + rec_ctx[1]).astype(h_ctx.dtype)
    y_ctx = jnp.concatenate([fourier_seq(f_ctx, w_four), jax.nn.gelu(g_ctx) * rec_c], axis=-1) @ w_out
    return y_lat, y_ctx


def swiglu(h, w_ffn_in, w_ffn_out):
    g, u = jnp.split(h @ w_ffn_in, 2, axis=-1)
    return (jax.nn.silu(g) * u) @ w_ffn_out


def setup_inputs(seed: int = 0) -> dict:
    key = jax.random.key(seed)
    ks = jax.random.split(key, 17)
    f32 = jnp.float32
    x = jax.random.normal(ks[0], (BATCH, SEQ, D_MODEL), f32)
    c = jax.random.normal(ks[1], (BATCH, D_MODEL), f32)
    ctx = jax.random.normal(ks[2], (BATCH, CTX_LEN, D_MODEL), f32)
    c_ctx = jax.random.normal(ks[3], (D_MODEL,), f32)
    w_ada = jax.random.normal(ks[4], (DEPTH, D_MODEL, N_MOD * D_MODEL), f32) * (0.5 * D_MODEL ** -0.5)
    b_ada = 0.02 * jax.random.normal(ks[5], (DEPTH, N_MOD * D_MODEL), f32)
    norm_g = 1.0 + 0.01 * jax.random.normal(ks[6], (DEPTH, N_NORMS, D_MODEL), f32)
    w_in = jax.random.normal(ks[7], (DEPTH, D_MODEL, IN_COLS), f32) * D_MODEL ** -0.5
    w_four = jax.random.normal(ks[8], (DEPTH, FOURIER_GROUPS, FOURIER_GROUP_DIM, FOURIER_GROUP_DIM), f32) * FOURIER_GROUP_DIM ** -0.5
    conv_w = jax.random.normal(ks[9], (DEPTH, CONV_WIDTH, LRU_WIDTH), f32) * CONV_WIDTH ** -0.5
    conv_b = 0.01 * jax.random.normal(ks[10], (DEPTH, LRU_WIDTH), f32)
    lru_w = jax.random.normal(ks[11], (DEPTH, 2, 2, LRU_HEADS, LRU_HEAD_DIM, LRU_HEAD_DIM), f32) * LRU_HEAD_DIM ** -0.5
    lru_b = 0.01 * jax.random.normal(ks[12], (DEPTH, 2, 2, LRU_WIDTH), f32)
    a_c = jax.random.uniform(ks[13], (DEPTH, 2, LRU_WIDTH), f32, minval=0.9, maxval=0.999)
    s = a_c ** (1.0 / LRU_C)
    lru_lam = jnp.log(s) - jnp.log1p(-s)
    w_out = jax.random.normal(ks[14], (DEPTH, MIX_WIDTH, D_MODEL), f32) * MIX_WIDTH ** -0.5
    w_ffn_in = jax.random.normal(ks[15], (DEPTH, D_MODEL, 2 * D_FF), f32) * D_MODEL ** -0.5
    w_ffn_out = jax.random.normal(ks[16], (DEPTH, D_FF, D_MODEL), f32) * D_FF ** -0.5
    return {"x": x, "c": c, "ctx": ctx, "c_ctx": c_ctx, "w_ada": w_ada, "b_ada": b_ada,
            "norm_g": norm_g, "w_in": w_in, "w_four": w_four, "conv_w": conv_w, "conv_b": conv_b,
            "lru_w": lru_w, "lru_b": lru_b, "lru_lam": lru_lam, "w_out": w_out,
            "w_ffn_in": w_ffn_in, "w_ffn_out": w_ffn_out}


def reference(x, c, ctx, c_ctx, w_ada, b_ada, norm_g, w_in, w_four, conv_w, conv_b,
              lru_w, lru_b, lru_lam, w_out, w_ffn_in, w_ffn_out):
    mod_lat = jnp.einsum("bd,ldm->lbm", jax.nn.silu(c), w_ada) + b_ada[:, None, :]
    mod_ctx = jnp.einsum("d,ldm->lm", jax.nn.silu(c_ctx), w_ada) + b_ada
    for layer in range(DEPTH):
        last = layer == DEPTH - 1
        ml = jnp.split(mod_lat[layer][:, None, :], N_MOD, axis=-1)
        mc = jnp.split(mod_ctx[layer][None, None, :], N_MOD, axis=-1)
        g = norm_g[layer]
        h_lat = modulate(rms_norm(x, g[0]), ml[0], ml[1])
        h_ctx = modulate(rms_norm(ctx, g[0]), mc[0], mc[1])
        y_lat, y_ctx = mixing_sublayer(h_lat, h_ctx, w_in[layer], w_four[layer], conv_w[layer], conv_b[layer],
                                       lru_w[layer], lru_b[layer], lru_lam[layer], w_out[layer],
                                       not last)
        x = x + ml[2] * rms_norm(y_lat, g[1])
        f_lat = swiglu(modulate(rms_norm(x, g[2]), ml[3], ml[4]), w_ffn_in[layer], w_ffn_out[layer])
        x = x + ml[5] * rms_norm(f_lat, g[3])
        if not last:
            ctx = ctx + mc[2] * rms_norm(y_ctx, g[1])
            f_ctx = swiglu(modulate(rms_norm(ctx, g[2]), mc[3], mc[4]), w_ffn_in[layer], w_ffn_out[layer])
            ctx = ctx + mc[5] * rms_norm(f_ctx, g[3])
    return x
```

```python
import functools
import math

import numpy as np
import jax
import jax.numpy as jnp
from jax import lax
from jax.experimental import pallas as pl
from jax.experimental.pallas import tpu as pltpu

GRID_W = 64
CONV_LEFT = 2
LRU_C = 8.0
EPS = 1e-6
N_MOD = 6

F32 = jnp.float32
BF16 = jnp.bfloat16

V7X_VMEM_LIMIT_BYTES = 56 * 1024 * 1024
SUBLANES = 8
LANES = 128
ROW_PITCH_PAD = 8


def _params(*sem):
    return pltpu.CompilerParams(dimension_semantics=sem, vmem_limit_bytes=V7X_VMEM_LIMIT_BYTES)


def _dot(a, b):
    return jnp.dot(a, b, preferred_element_type=F32)


def _tile(n, pref):
    t = min(n, pref)
    while n % t:
        t //= 2
    return t


def _rms(x, gain):
    ms = jnp.mean(x * x, axis=-1, keepdims=True)
    return x * lax.rsqrt(ms + EPS) * gain


def _resident(shape):
    nd = len(shape)
    return pl.BlockSpec(shape, lambda *_: (0,) * nd, pipeline_mode=pl.Buffered(1))


def _ada_kernel(c_ref, w_ref, b_ref, o_ref):
    c = c_ref[...]
    s = (c * jax.nn.sigmoid(c)).astype(BF16)
    o_ref[...] = _dot(s, w_ref[...].astype(BF16)) + b_ref[...]


def _ada(cs, w_ada, b_ada):
    depth, d, nm = w_ada.shape
    rows = cs.shape[0]
    tn = _tile(nm, 1024)
    return pl.pallas_call(
        _ada_kernel,
        grid=(depth, nm // tn),
        in_specs=[
            pl.BlockSpec((rows, d), lambda l, j: (0, 0)),
            pl.BlockSpec((None, d, tn), lambda l, j: (l, 0, j)),
            pl.BlockSpec((None, 1, tn), lambda l, j: (l, 0, j)),
        ],
        out_specs=pl.BlockSpec((None, rows, tn), lambda l, j: (l, 0, j)),
        out_shape=jax.ShapeDtypeStruct((depth, rows, nm), F32),
        compiler_params=_params("arbitrary", "arbitrary"),
        name="ada_mod",
    )(cs, w_ada, b_ada.reshape(depth, 1, nm))


def _inproj_kernel(x_ref, sh_ref, sc_ref, gain_ref, w_ref, f_ref, r_ref, g_ref):
    h = _rms(x_ref[...], gain_ref[...])
    h = (h * (1.0 + sc_ref[...]) + sh_ref[...]).astype(BF16)
    nf = f_ref.shape[-1]
    nr = r_ref.shape[-1]
    f_ref[...] = _dot(h, w_ref[:, :nf]).astype(f_ref.dtype)
    r_ref[...] = _dot(h, w_ref[:, nf:nf + nr]).astype(r_ref.dtype)
    g_ref[...] = _dot(h, w_ref[:, nf + nr:]).astype(g_ref.dtype)


def _inproj(x, shift, scale, gain, w_in, fw, lw):
    b, l, d = x.shape
    tm = _tile(l, 512)
    row = lambda bi, i: (bi, i, 0)
    vec = pl.BlockSpec((None, 1, d), lambda bi, i: (bi, 0, 0))
    return pl.pallas_call(
        _inproj_kernel,
        grid=(b, l // tm),
        in_specs=[pl.BlockSpec((None, tm, d), row), vec, vec, _resident((1, d)),
                  _resident(w_in.shape)],
        out_specs=[pl.BlockSpec((None, tm, fw), row), pl.BlockSpec((None, tm, lw), row),
                   pl.BlockSpec((None, tm, lw), row)],
        out_shape=[jax.ShapeDtypeStruct((b, l, fw), BF16), jax.ShapeDtypeStruct((b, l, lw), F32),
                   jax.ShapeDtypeStruct((b, l, lw), BF16)],
        compiler_params=_params("arbitrary", "arbitrary"),
        name="in_proj",
    )(x, shift, scale, gain.reshape(1, d), w_in)


def _dft(n):
    k = np.arange(n)
    ang = 2.0 * np.pi * ((k[:, None] * k[None, :]) % n) / n
    return np.cos(ang) / math.sqrt(n), np.sin(ang) / math.sqrt(n)


def _fold_channel_dft(ck_ref, sk_ref, wf_ref):
    wf = wf_ref[...]
    return _dot(ck_ref[...], wf).astype(BF16), (-_dot(sk_ref[...], wf)).astype(BF16)


def _fourier_grid_kernel(u_ref, mc_ref, mr_ref, ck_ref, sk_ref, wf_ref, o_ref, pc_ref, ps_ref, *,
                         rows, cols, pitch):
    gd = u_ref.shape[-1]
    nslab = gd // LANES
    wa, wb = _fold_channel_dft(ck_ref, sk_ref, wf_ref)
    mc = mc_ref[...]
    mr = mr_ref[...]

    def col_stage(r, carry):
        src = pl.multiple_of(r * cols, cols)
        dst = pl.multiple_of(r * pitch, SUBLANES)
        t = _dot(mc, u_ref[pl.ds(src, cols), :])
        for s in range(nslab):
            lanes = slice(s * LANES, (s + 1) * LANES)
            pc_ref[s, pl.ds(dst, cols), :] = t[:cols, lanes]
            ps_ref[s, pl.ds(dst, cols), :] = t[cols:, lanes]
        return carry

    lax.fori_loop(0, rows, col_stage, 0)

    def row_stage(q, carry):
        parts_c = [pc_ref.at[s][pl.ds(q, rows, stride=pitch), :] for s in range(nslab)]
        parts_s = [ps_ref.at[s][pl.ds(q, rows, stride=pitch), :] for s in range(nslab)]
        xq = jnp.concatenate([jnp.concatenate(parts_c, axis=-1),
                              jnp.concatenate(parts_s, axis=-1)], axis=0).astype(BF16)
        y = _dot(mr, xq)
        z = _dot(y[:rows].astype(BF16), wa) + _dot(y[rows:].astype(BF16), wb)
        for s in range(nslab):
            pc_ref.at[s][pl.ds(q, rows, stride=pitch), :] = z[:, s * LANES:(s + 1) * LANES]
        return carry

    lax.fori_loop(0, cols, row_stage, 0)

    def emit(r, carry):
        src = pl.multiple_of(r * pitch, SUBLANES)
        dst = pl.multiple_of(r * cols, cols)
        for s in range(nslab):
            o_ref[pl.ds(dst, cols), s * LANES:(s + 1) * LANES] = (
                pc_ref[s, pl.ds(src, cols), :].astype(o_ref.dtype))
        return carry

    lax.fori_loop(0, rows, emit, 0)


def _fourier_seq_kernel(u_ref, ml_ref, ck_ref, sk_ref, wf_ref, o_ref):
    l = u_ref.shape[0]
    wa, wb = _fold_channel_dft(ck_ref, sk_ref, wf_ref)
    y = _dot(ml_ref[...], u_ref[...])
    z = _dot(y[:l].astype(BF16), wa) + _dot(y[l:].astype(BF16), wb)
    o_ref[...] = z.astype(o_ref.dtype)


def _fourier(u, w_four, grid_w):
    b, l, fw = u.shape
    g, gd, _ = w_four.shape
    ck, sk = _dft(gd)
    ck, sk = jnp.asarray(ck, BF16), jnp.asarray(sk, BF16)
    blk = pl.BlockSpec((None, l, gd), lambda bi, gi: (bi, 0, gi))
    wspec = pl.BlockSpec((None, gd, gd), lambda bi, gi: (gi, 0, 0))
    common = dict(
        grid=(b, g),
        out_specs=blk,
        out_shape=jax.ShapeDtypeStruct((b, l, fw), BF16),
        compiler_params=_params("arbitrary", "arbitrary"),
    )
    if grid_w is None:
        cl, sl = _dft(l)
        ml = jnp.asarray(np.concatenate([cl, sl], axis=0), BF16)
        return pl.pallas_call(
            _fourier_seq_kernel,
            in_specs=[blk, _resident(ml.shape), _resident(ck.shape), _resident(sk.shape), wspec],
            name="fourier_seq", **common,
        )(u, ml, ck, sk, w_four)
    cols = grid_w
    rows = l // cols
    pitch = cols + ROW_PITCH_PAD
    cc, sc = _dft(cols)
    cr, sr = _dft(rows)
    mc = jnp.asarray(np.concatenate([cc, sc], axis=0), BF16)
    mr = jnp.asarray(np.block([[cr, -sr], [sr, cr]]), BF16)
    scratch = pltpu.VMEM((gd // LANES, rows * pitch, LANES), F32)
    return pl.pallas_call(
        functools.partial(_fourier_grid_kernel, rows=rows, cols=cols, pitch=pitch),
        in_specs=[blk, _resident(mc.shape), _resident(mr.shape), _resident(ck.shape),
                  _resident(sk.shape), wspec],
        scratch_shapes=[scratch, scratch],
        name="fourier_grid", **common,
    )(u, mc, mr, ck, sk, w_four)


def _rglru_kernel(*refs, reverse, gated, heads):
    if gated:
        (r_ref, rp_ref, rn_ref, cw_ref, cb_ref, wg_ref, bg_ref, lam_ref, h0_ref, other_ref, g_ref,
         o_ref, hfin_ref, a_sc, b_sc, carry) = refs
    else:
        (r_ref, rp_ref, rn_ref, cw_ref, cb_ref, wg_ref, bg_ref, lam_ref, h0_ref,
         o_ref, hfin_ref, a_sc, b_sc, carry) = refs
    t, w = r_ref.shape
    hd = w // heads
    i = pl.program_id(1)
    n = pl.num_programs(1)
    ci = (n - 1 - i) if reverse else i

    @pl.when(i == 0)
    def _():
        carry[...] = h0_ref[...]

    prev = jnp.where(ci == 0, 0.0, rp_ref[...].astype(F32))
    nxt = jnp.where(ci == n - 1, 0.0, rn_ref[...].astype(F32))
    xp = jnp.concatenate([prev, r_ref[...].astype(F32), nxt], axis=0)
    u = cb_ref[...]
    for k in range(cw_ref.shape[0]):
        off = SUBLANES - CONV_LEFT + k
        u = u + cw_ref[k:k + 1, :] * xp[off:off + t]
    ub = u.astype(BF16)

    clam = -LRU_C * jax.nn.softplus(-lam_ref[...])
    for h in range(heads):
        ch = slice(h * hd, (h + 1) * hd)
        gates = _dot(ub[:, ch], wg_ref[h])
        rg = jax.nn.sigmoid(gates[:, :hd] + bg_ref[0:1, ch])
        ig = jax.nn.sigmoid(gates[:, hd:] + bg_ref[1:2, ch])
        log_a = rg * clam[:, ch]
        a = jnp.exp(log_a)
        a_sc[:, ch] = a
        b_sc[:, ch] = jnp.sqrt((1.0 - a) * (1.0 + a)) * (ig * u[:, ch])

    row = lax.broadcasted_iota(jnp.int32, (SUBLANES, w), 0)
    ngroups = t // SUBLANES

    def group(gi, hprev):
        gpos = (ngroups - 1 - gi) if reverse else gi
        base = pl.multiple_of(gpos * SUBLANES, SUBLANES)
        a = a_sc[pl.ds(base, SUBLANES), :]
        bx = b_sc[pl.ds(base, SUBLANES), :]
        for s in (1, 2, 4):
            shift = (SUBLANES - s) if reverse else s
            valid = (row < SUBLANES - s) if reverse else (row >= s)
            a_p = pltpu.roll(a, shift, 0)
            b_p = pltpu.roll(bx, shift, 0)
            bx = jnp.where(valid, a * b_p + bx, bx)
            a = jnp.where(valid, a * a_p, a)
        hh = a * hprev + bx
        b_sc[pl.ds(base, SUBLANES), :] = hh
        last = hh[0:1] if reverse else hh[SUBLANES - 1:SUBLANES]
        return jnp.broadcast_to(last, (SUBLANES, w))

    hfin = lax.fori_loop(0, ngroups, group, carry[...])
    carry[...] = hfin
    hfin_ref[...] = hfin
    if gated:
        rec = (other_ref[...] + b_sc[...])
        o_ref[...] = (jax.nn.gelu(g_ref[...].astype(F32)) * rec).astype(o_ref.dtype)
    else:
        o_ref[...] = b_sc[...]


def _rglru(r, conv_w, conv_b, wg, bg, lam, h0, reverse, other=None, gate=None):
    b, l, w = r.shape
    heads, hd, _ = wg.shape
    t = _tile(l, 512)
    n = l // t
    tb = t // SUBLANES
    nb = l // SUBLANES
    gated = other is not None

    def pos(i):
        return (n - 1 - i) if reverse else i

    main = lambda bi, i: (bi, pos(i), 0)
    in_specs = [
        pl.BlockSpec((None, t, w), main),
        pl.BlockSpec((None, SUBLANES, w), lambda bi, i: (bi, jnp.maximum(pos(i) * tb - 1, 0), 0)),
        pl.BlockSpec((None, SUBLANES, w), lambda bi, i: (bi, jnp.minimum((pos(i) + 1) * tb, nb - 1), 0)),
        _resident(conv_w.shape), _resident((1, w)), _resident(wg.shape), _resident(bg.shape),
        _resident((1, w)),
        pl.BlockSpec((None, SUBLANES, w), lambda bi, i: (bi, 0, 0)),
    ]
    args = [r, r, r, conv_w, conv_b.reshape(1, w), wg, bg, lam.reshape(1, w), h0]
    if gated:
        in_specs += [pl.BlockSpec((None, t, w), main), pl.BlockSpec((None, t, w), main)]
        args += [other, gate]
    return pl.pallas_call(
        functools.partial(_rglru_kernel, reverse=reverse, gated=gated, heads=heads),
        grid=(b, n),
        in_specs=in_specs,
        out_specs=[pl.BlockSpec((None, t, w), main),
                   pl.BlockSpec((None, SUBLANES, w), lambda bi, i: (bi, 0, 0))],
        out_shape=[jax.ShapeDtypeStruct((b, l, w), BF16 if gated else F32),
                   jax.ShapeDtypeStruct((b, SUBLANES, w), F32)],
        scratch_shapes=[pltpu.VMEM((t, w), F32), pltpu.VMEM((t, w), F32), pltpu.VMEM((SUBLANES, w), F32)],
        compiler_params=_params("arbitrary", "arbitrary"),
        name="rglru_bwd" if reverse else "rglru_fwd",
    )(*args)


def _outproj_kernel(four_ref, rec_ref, w_ref, x_ref, gate_ref, g_post_ref, g_pre_ref, sh_ref, sc_ref,
                    xo_ref, h_ref):
    lhs = jnp.concatenate([four_ref[...], rec_ref[...]], axis=-1)
    y = _dot(lhs, w_ref[...])
    xn = x_ref[...] + gate_ref[...] * _rms(y, g_post_ref[...])
    xo_ref[...] = xn
    h = _rms(xn, g_pre_ref[...])
    h_ref[...] = (h * (1.0 + sc_ref[...]) + sh_ref[...]).astype(h_ref.dtype)


def _outproj(four, rec, w_out, x, gate, g_post, g_pre, shift, scale):
    b, l, d = x.shape
    fw, lw = four.shape[-1], rec.shape[-1]
    tm = _tile(l, 512)
    row = lambda bi, i: (bi, i, 0)
    vec = pl.BlockSpec((None, 1, d), lambda bi, i: (bi, 0, 0))
    return pl.pallas_call(
        _outproj_kernel,
        grid=(b, l // tm),
        in_specs=[pl.BlockSpec((None, tm, fw), row), pl.BlockSpec((None, tm, lw), row),
                  _resident(w_out.shape), pl.BlockSpec((None, tm, d), row), vec,
                  _resident((1, d)), _resident((1, d)), vec, vec],
        out_specs=[pl.BlockSpec((None, tm, d), row), pl.BlockSpec((None, tm, d), row)],
        out_shape=[jax.ShapeDtypeStruct((b, l, d), F32), jax.ShapeDtypeStruct((b, l, d), BF16)],
        compiler_params=_params("arbitrary", "arbitrary"),
        name="out_proj",
    )(four, rec, w_out, x, gate, g_post.reshape(1, d), g_pre.reshape(1, d), shift, scale)


def _ffn_kernel(h_ref, wg_ref, wu_ref, wo_ref, x_ref, gate_ref, gain_ref, o_ref, acc_ref):
    j = pl.program_id(2)

    @pl.when(j == 0)
    def _():
        acc_ref[...] = jnp.zeros_like(acc_ref)

    h = h_ref[...]
    g = _dot(h, wg_ref[...])
    u = _dot(h, wu_ref[...])
    act = (g * jax.nn.sigmoid(g) * u).astype(BF16)
    acc_ref[...] += _dot(act, wo_ref[...])

    @pl.when(j == pl.num_programs(2) - 1)
    def _():
        o_ref[...] = x_ref[...] + gate_ref[...] * _rms(acc_ref[...], gain_ref[...])


def _ffn(h, w_ffn_in, w_ffn_out, x, gate, gain):
    b, l, d = x.shape
    dff = w_ffn_out.shape[0]
    tm = _tile(l, 512)
    tf = _tile(dff, 512)
    nj = dff // tf
    row = lambda bi, i, j: (bi, i, 0)
    return pl.pallas_call(
        _ffn_kernel,
        grid=(b, l // tm, nj),
        in_specs=[pl.BlockSpec((None, tm, d), row),
                  pl.BlockSpec((d, tf), lambda bi, i, j: (0, j)),
                  pl.BlockSpec((d, tf), lambda bi, i, j: (0, nj + j)),
                  pl.BlockSpec((tf, d), lambda bi, i, j: (j, 0)),
                  pl.BlockSpec((None, tm, d), row),
                  pl.BlockSpec((None, 1, d), lambda bi, i, j: (bi, 0, 0)),
                  _resident((1, d))],
        out_specs=pl.BlockSpec((None, tm, d), row),
        out_shape=jax.ShapeDtypeStruct((b, l, d), F32),
        scratch_shapes=[pltpu.VMEM((tm, d), F32)],
        compiler_params=_params("arbitrary", "arbitrary", "arbitrary"),
        name="ffn",
    )(h, w_ffn_in, w_ffn_in, w_ffn_out, x, gate, gain.reshape(1, d))


def kernel(x, c, ctx, c_ctx, w_ada, b_ada, norm_g, w_in, w_four, conv_w, conv_b, lru_w, lru_b, lru_lam,
           w_out, w_ffn_in, w_ffn_out):
    depth = w_ada.shape[0]
    bsz, _, d = x.shape
    fw = w_four.shape[1] * w_four.shape[2]
    lw = conv_w.shape[-1]

    pad = (-(bsz + 1)) % SUBLANES
    cs = jnp.concatenate([c, c_ctx[None, :], jnp.zeros((pad, d), F32)], axis=0)
    mods = _ada(cs, w_ada, b_ada)

    w_in_b = w_in.astype(BF16)
    w_out_b = w_out.astype(BF16)
    w_ffn_in_b = w_ffn_in.astype(BF16)
    w_ffn_out_b = w_ffn_out.astype(BF16)
    w_four_b = w_four.astype(BF16)
    w_gate_b = jnp.concatenate([lru_w[:, :, 0], lru_w[:, :, 1]], axis=-1).astype(BF16)
    zero_state = jnp.zeros((bsz, SUBLANES, lw), F32)

    for layer in range(depth):
        last = layer == depth - 1
        ml = [m[:, None, :] for m in jnp.split(mods[layer, :bsz], N_MOD, axis=-1)]
        mc = [jnp.broadcast_to(m[None, None, :], (bsz, 1, d))
              for m in jnp.split(mods[layer, bsz], N_MOD, axis=-1)]
        g = norm_g[layer]

        f_lat, r_lat, g_lat = _inproj(x, ml[0], ml[1], g[0], w_in_b[layer], fw, lw)
        f_ctx, r_ctx, g_ctx = _inproj(ctx, mc[0], mc[1], g[0], w_in_b[layer], fw, lw)

        def scan(r, d_idx, h0, other=None, gate=None):
            return _rglru(r, conv_w[layer], conv_b[layer], w_gate_b[layer, d_idx], lru_b[layer, d_idx],
                          lru_lam[layer, d_idx], h0, bool(d_idx), other, gate)

        recf_ctx, hf = scan(r_ctx, 0, zero_state)
        if last:
            _, hb = scan(r_ctx, 1, zero_state)
        else:
            rec_ctx, hb = scan(r_ctx, 1, zero_state, recf_ctx, g_ctx)
        recf_lat, _ = scan(r_lat, 0, hf)
        rec_lat, _ = scan(r_lat, 1, hb, recf_lat, g_lat)

        four_lat = _fourier(f_lat, w_four_b[layer], GRID_W)
        x, h_lat = _outproj(four_lat, rec_lat, w_out_b[layer], x, ml[2], g[1], g[2], ml[3], ml[4])
        x = _ffn(h_lat, w_ffn_in_b[layer], w_ffn_out_b[layer], x, ml[5], g[3])
        if not last:
            four_ctx = _fourier(f_ctx, w_four_b[layer], None)
            ctx, h_ctx = _outproj(four_ctx, rec_ctx, w_out_b[layer], ctx, mc[2], g[1], g[2], mc[3], mc[4])
            ctx = _ffn(h_ctx, w_ffn_in_b[layer], w_ffn_out_b[layer], ctx, mc[5], g[3])
    return x
```

```python
import functools
import math

import numpy as np
import jax
import jax.numpy as jnp
from jax import lax
from jax.experimental import pallas as pl
from jax.experimental.pallas import tpu as pltpu

GRID_W = 64
CONV_LEFT = 2
LRU_C = 8.0
EPS = 1e-6
N_MOD = 6

F32 = jnp.float32
BF16 = jnp.bfloat16

V7X_VMEM_LIMIT_BYTES = 56 * 1024 * 1024
SUBLANES = 8
LANES = 128
ROW_PITCH_PAD = 8
FOURIER_COL_UNROLL = 8
FOURIER_Q_BATCH = 4
ROW_TILE = 512
FFN_ROW_TILE = 1024
FFN_SUB_ROWS = 512
FFN_COL_TILE = 512
SCAN_CHUNK = 512
SQRT_GUARD = 1e-30


def _params(*sem, vmem=V7X_VMEM_LIMIT_BYTES):
    return pltpu.CompilerParams(dimension_semantics=sem, vmem_limit_bytes=vmem)


def _dot(a, b):
    return jnp.dot(a, b, preferred_element_type=F32)


def _tile(n, pref):
    t = min(n, pref)
    while n % t:
        t //= 2
    return t


def _rms(x, gain):
    ms = jnp.mean(x * x, axis=-1, keepdims=True)
    return x * lax.rsqrt(ms + EPS) * gain


def _resident(shape):
    nd = len(shape)
    return pl.BlockSpec(shape, lambda *_: (0,) * nd, pipeline_mode=pl.Buffered(1))


def _layer_block(arr, *lead):
    nlead = len(lead)
    rest = arr.shape[nlead:]
    idx = tuple(lead) + (0,) * len(rest)
    return pl.BlockSpec((None,) * nlead + tuple(rest), lambda *_: idx, pipeline_mode=pl.Buffered(1))


def _mod_spec(mods, layer, k, d, row_of):
    return pl.BlockSpec((None, None, 1, d), lambda bi, *_: (layer, row_of(bi), 0, k))


def _ada_kernel(c_ref, w_ref, b_ref, o_ref):
    c = c_ref[...]
    s = (c * jax.nn.sigmoid(c)).astype(BF16)
    o_ref[...] = _dot(s, w_ref[...].astype(BF16)) + b_ref[...]


def _ada(cs, w_ada, b_ada):
    depth, d, nm = w_ada.shape
    rows = cs.shape[0]
    tn = _tile(nm, 1024)
    return pl.pallas_call(
        _ada_kernel,
        grid=(depth, nm // tn),
        in_specs=[
            pl.BlockSpec((rows, d), lambda l, j: (0, 0)),
            pl.BlockSpec((None, d, tn), lambda l, j: (l, 0, j)),
            pl.BlockSpec((None, 1, tn), lambda l, j: (l, 0, j)),
        ],
        out_specs=pl.BlockSpec((None, rows, tn), lambda l, j: (l, 0, j)),
        out_shape=jax.ShapeDtypeStruct((depth, rows, nm), F32),
        compiler_params=_params("arbitrary", "arbitrary"),
        name="ada_mod",
    )(cs, w_ada, b_ada.reshape(depth, 1, nm))


def _inproj_kernel(x_ref, sh_ref, sc_ref, gain_ref, w_ref, f_ref, r_ref, g_ref):
    h = _rms(x_ref[...], gain_ref[...])
    h = (h * (1.0 + sc_ref[...]) + sh_ref[...]).astype(BF16)
    nf = f_ref.shape[-1]
    nr = r_ref.shape[-1]
    f_ref[...] = _dot(h, w_ref[:, :nf]).astype(f_ref.dtype)
    r_ref[...] = _dot(h, w_ref[:, nf:nf + nr]).astype(r_ref.dtype)
    g_ref[...] = _dot(h, w_ref[:, nf + nr:]).astype(g_ref.dtype)


def _inproj(x, mods, row_of, gains, w_in, layer, fw, lw):
    b, l, d = x.shape
    tm = _tile(l, ROW_TILE)
    row = lambda bi, i: (bi, i, 0)
    return pl.pallas_call(
        _inproj_kernel,
        grid=(b, l // tm),
        in_specs=[pl.BlockSpec((None, tm, d), row),
                  _mod_spec(mods, layer, 0, d, row_of), _mod_spec(mods, layer, 1, d, row_of),
                  _layer_block(gains, layer, 0), _layer_block(w_in, layer)],
        out_specs=[pl.BlockSpec((None, tm, fw), row), pl.BlockSpec((None, tm, lw), row),
                   pl.BlockSpec((None, tm, lw), row)],
        out_shape=[jax.ShapeDtypeStruct((b, l, fw), BF16), jax.ShapeDtypeStruct((b, l, lw), F32),
                   jax.ShapeDtypeStruct((b, l, lw), BF16)],
        compiler_params=_params("arbitrary", "arbitrary"),
        name="in_proj",
    )(x, mods, mods, gains, w_in)


def _dft(n):
    k = np.arange(n)
    ang = 2.0 * np.pi * ((k[:, None] * k[None, :]) % n) / n
    return np.cos(ang) / math.sqrt(n), np.sin(ang) / math.sqrt(n)


def _fold_channel_dft(ck_ref, sk_ref, wf_ref):
    wf = wf_ref[...]
    return _dot(ck_ref[...], wf).astype(BF16), (-_dot(sk_ref[...], wf)).astype(BF16)


def _fourier_grid_kernel(u_ref, mc_ref, mr_ref, ck_ref, sk_ref, wf_ref, o_ref, pc_ref, ps_ref, *,
                         rows, cols, pitch):
    gd = u_ref.shape[-1]
    nslab = gd // LANES
    wa, wb = _fold_channel_dft(ck_ref, sk_ref, wf_ref)
    wab = jnp.concatenate([wa, wb], axis=0)
    mc = mc_ref[...]
    mr = mr_ref[...]

    def col_stage(r, carry):
        src = pl.multiple_of(r * cols, cols)
        dst = pl.multiple_of(r * pitch, SUBLANES)
        t = _dot(mc, u_ref[pl.ds(src, cols), :])
        for s in range(nslab):
            lanes = slice(s * LANES, (s + 1) * LANES)
            pc_ref[s, pl.ds(dst, cols), :] = t[:cols, lanes]
            ps_ref[s, pl.ds(dst, cols), :] = t[cols:, lanes]
        return carry

    lax.fori_loop(0, rows, col_stage, 0, unroll=FOURIER_COL_UNROLL)

    def gather(ref, q):
        return jnp.concatenate([ref.at[s][pl.ds(q, rows, stride=pitch), :] for s in range(nslab)],
                               axis=-1).astype(BF16)

    def row_stage(it, carry):
        q0 = it * FOURIER_Q_BATCH
        x = jnp.concatenate(
            [jnp.concatenate([gather(pc_ref, q0 + k), gather(ps_ref, q0 + k)], axis=0)
             for k in range(FOURIER_Q_BATCH)], axis=-1)
        y = _dot(mr, x).astype(BF16)
        lhs = jnp.concatenate(
            [jnp.concatenate([y[:rows, k * gd:(k + 1) * gd], y[rows:, k * gd:(k + 1) * gd]], axis=-1)
             for k in range(FOURIER_Q_BATCH)], axis=0)
        z = _dot(lhs, wab)
        for k in range(FOURIER_Q_BATCH):
            for s in range(nslab):
                pc_ref.at[s][pl.ds(q0 + k, rows, stride=pitch), :] = (
                    z[k * rows:(k + 1) * rows, s * LANES:(s + 1) * LANES])
        return carry

    lax.fori_loop(0, cols // FOURIER_Q_BATCH, row_stage, 0)

    def emit(r, carry):
        src = pl.multiple_of(r * pitch, SUBLANES)
        dst = pl.multiple_of(r * cols, cols)
        for s in range(nslab):
            o_ref[pl.ds(dst, cols), s * LANES:(s + 1) * LANES] = (
                pc_ref[s, pl.ds(src, cols), :].astype(o_ref.dtype))
        return carry

    lax.fori_loop(0, rows, emit, 0, unroll=FOURIER_COL_UNROLL)


def _fourier_seq_kernel(u_ref, ml_ref, ck_ref, sk_ref, wf_ref, o_ref):
    l = u_ref.shape[0]
    wa, wb = _fold_channel_dft(ck_ref, sk_ref, wf_ref)
    y = _dot(ml_ref[...], u_ref[...])
    z = _dot(y[:l].astype(BF16), wa) + _dot(y[l:].astype(BF16), wb)
    o_ref[...] = z.astype(o_ref.dtype)


def _fourier(u, w_four, layer, grid_w):
    b, l, fw = u.shape
    _, g, gd, _ = w_four.shape
    ck, sk = _dft(gd)
    ck, sk = jnp.asarray(ck, BF16), jnp.asarray(sk, BF16)
    blk = pl.BlockSpec((None, l, gd), lambda bi, gi: (bi, 0, gi))
    wspec = pl.BlockSpec((None, None, gd, gd), lambda bi, gi: (layer, gi, 0, 0))
    common = dict(
        grid=(b, g),
        out_specs=blk,
        out_shape=jax.ShapeDtypeStruct((b, l, fw), BF16),
        compiler_params=_params("arbitrary", "arbitrary"),
    )
    if grid_w is None:
        cl, sl = _dft(l)
        ml = jnp.asarray(np.concatenate([cl, sl], axis=0), BF16)
        return pl.pallas_call(
            _fourier_seq_kernel,
            in_specs=[blk, _resident(ml.shape), _resident(ck.shape), _resident(sk.shape), wspec],
            name="fourier_seq", **common,
        )(u, ml, ck, sk, w_four)
    cols = grid_w
    rows = l // cols
    pitch = cols + ROW_PITCH_PAD
    cc, sc = _dft(cols)
    cr, sr = _dft(rows)
    mc = jnp.asarray(np.concatenate([cc, sc], axis=0), BF16)
    mr = jnp.asarray(np.block([[cr, -sr], [sr, cr]]), BF16)
    scratch = pltpu.VMEM((gd // LANES, rows * pitch, LANES), F32)
    return pl.pallas_call(
        functools.partial(_fourier_grid_kernel, rows=rows, cols=cols, pitch=pitch),
        in_specs=[blk, _resident(mc.shape), _resident(mr.shape), _resident(ck.shape),
                  _resident(sk.shape), wspec],
        scratch_shapes=[scratch, scratch],
        name="fourier_grid", **common,
    )(u, mc, mr, ck, sk, w_four)


def _rglru_kernel(*refs, reverse, conv, gated, heads):
    refs = list(refs)
    if conv:
        r_ref, rp_ref, rn_ref, cw_ref, cb_ref = refs[:5]
        del refs[:5]
    else:
        u_ref = refs.pop(0)
    wg_ref, bg_ref, lam_ref, h0_ref = refs[:4]
    del refs[:4]
    if gated:
        other_ref, g_ref = refs[:2]
        del refs[:2]
    o_ref = refs.pop(0)
    if conv:
        uo_ref = refs.pop(0)
    hfin_ref = refs.pop(0)
    if conv:
        xp_sc = refs.pop(0)
    a_sc, b_sc, carry = refs

    t, w = o_ref.shape
    hd = w // heads
    i = pl.program_id(1)
    n = pl.num_programs(1)
    ci = (n - 1 - i) if reverse else i

    @pl.when(i == 0)
    def _():
        carry[...] = h0_ref[...]

    if conv:
        xp_sc[0:SUBLANES, :] = jnp.where(ci == 0, 0.0, rp_ref[...])
        xp_sc[SUBLANES:SUBLANES + t, :] = r_ref[...]
        xp_sc[SUBLANES + t:, :] = jnp.where(ci == n - 1, 0.0, rn_ref[...])
        u = cb_ref[...]
        for k in range(cw_ref.shape[0]):
            off = SUBLANES - CONV_LEFT + k
            u = u + cw_ref[k:k + 1, :] * xp_sc[off:off + t, :]
        uo_ref[...] = u
    else:
        u = u_ref[...]
    ub = u.astype(BF16)

    clam = -LRU_C * jax.nn.softplus(-lam_ref[...])
    for h in range(heads):
        ch = slice(h * hd, (h + 1) * hd)
        gates = _dot(ub[:, ch], wg_ref[h])
        rg = jax.nn.sigmoid(gates[:, :hd] + bg_ref[0:1, ch])
        ig = jax.nn.sigmoid(gates[:, hd:] + bg_ref[1:2, ch])
        log_a = rg * clam[:, ch]
        a = jnp.exp(log_a)
        a_sc[:, ch] = a
        m = jnp.maximum((1.0 - a) * (1.0 + a), 0.0)
        b_sc[:, ch] = (m * lax.rsqrt(jnp.maximum(m, SQRT_GUARD))) * (ig * u[:, ch])

    row = lax.broadcasted_iota(jnp.int32, (SUBLANES, w), 0)
    ngroups = t // SUBLANES

    def group(gi, hprev):
        gpos = (ngroups - 1 - gi) if reverse else gi
        base = pl.multiple_of(gpos * SUBLANES, SUBLANES)
        a = a_sc[pl.ds(base, SUBLANES), :]
        bx = b_sc[pl.ds(base, SUBLANES), :]
        for s in (1, 2, 4):
            shift = (SUBLANES - s) if reverse else s
            valid = (row < SUBLANES - s) if reverse else (row >= s)
            a_p = pltpu.roll(a, shift, 0)
            b_p = pltpu.roll(bx, shift, 0)
            bx = jnp.where(valid, a * b_p + bx, bx)
            a = jnp.where(valid, a * a_p, a)
        hh = a * hprev + bx
        b_sc[pl.ds(base, SUBLANES), :] = hh
        last = hh[0:1] if reverse else hh[SUBLANES - 1:SUBLANES]
        return jnp.broadcast_to(last, (SUBLANES, w))

    hfin = lax.fori_loop(0, ngroups, group, carry[...])
    carry[...] = hfin
    hfin_ref[...] = hfin
    if gated:
        rec = (other_ref[...] + b_sc[...])
        o_ref[...] = (jax.nn.gelu(g_ref[...].astype(F32)) * rec).astype(o_ref.dtype)
    else:
        o_ref[...] = b_sc[...]


def _rglru(src, layer, direction, params, h0, other=None, gate=None):
    conv_w, conv_b, w_gate, b_gate, lam = params
    b, l, w = src.shape
    heads = w_gate.shape[2]
    reverse = bool(direction)
    conv = not reverse
    gated = other is not None
    t = _tile(l, SCAN_CHUNK)
    n = l // t
    tb = t // SUBLANES
    nb = l // SUBLANES

    def pos(i):
        return (n - 1 - i) if reverse else i

    main = pl.BlockSpec((None, t, w), lambda bi, i: (bi, pos(i), 0))
    state = pl.BlockSpec((None, SUBLANES, w), lambda bi, i: (bi, 0, 0))
    in_specs, args = [main], [src]
    if conv:
        in_specs += [
            pl.BlockSpec((None, SUBLANES, w), lambda bi, i: (bi, jnp.maximum(pos(i) * tb - 1, 0), 0)),
            pl.BlockSpec((None, SUBLANES, w), lambda bi, i: (bi, jnp.minimum((pos(i) + 1) * tb, nb - 1), 0)),
            _layer_block(conv_w, layer), _layer_block(conv_b, layer)]
        args += [src, src, conv_w, conv_b]
    in_specs += [_layer_block(w_gate, layer, direction), _layer_block(b_gate, layer, direction),
                 _layer_block(lam, layer, direction), state]
    args += [w_gate, b_gate, lam, h0]
    if gated:
        in_specs += [main, main]
        args += [other, gate]
    out_specs = [main] + ([main] if conv else []) + [state]
    out_shape = ([jax.ShapeDtypeStruct((b, l, w), BF16 if gated else F32)]
                 + ([jax.ShapeDtypeStruct((b, l, w), F32)] if conv else [])
                 + [jax.ShapeDtypeStruct((b, SUBLANES, w), F32)])
    scratch = ([pltpu.VMEM((t + 2 * SUBLANES, w), F32)] if conv else []) + [
        pltpu.VMEM((t, w), F32), pltpu.VMEM((t, w), F32), pltpu.VMEM((SUBLANES, w), F32)]
    return pl.pallas_call(
        functools.partial(_rglru_kernel, reverse=reverse, conv=conv, gated=gated, heads=heads),
        grid=(b, n),
        in_specs=in_specs,
        out_specs=out_specs,
        out_shape=out_shape,
        scratch_shapes=scratch,
        compiler_params=_params("arbitrary", "arbitrary"),
        name="rglru_bwd" if reverse else "rglru_fwd",
    )(*args)


def _outproj_kernel(four_ref, rec_ref, w_ref, x_ref, gate_ref, g_post_ref, g_pre_ref, sh_ref, sc_ref,
                    xo_ref, h_ref):
    lhs = jnp.concatenate([four_ref[...], rec_ref[...]], axis=-1)
    y = _dot(lhs, w_ref[...])
    xn = x_ref[...] + gate_ref[...] * _rms(y, g_post_ref[...])
    xo_ref[...] = xn
    h = _rms(xn, g_pre_ref[...])
    h_ref[...] = (h * (1.0 + sc_ref[...]) + sh_ref[...]).astype(h_ref.dtype)


def _outproj(four, rec, w_out, x, mods, row_of, gains, layer):
    b, l, d = x.shape
    fw, lw = four.shape[-1], rec.shape[-1]
    tm = _tile(l, ROW_TILE)
    row = lambda bi, i: (bi, i, 0)
    return pl.pallas_call(
        _outproj_kernel,
        grid=(b, l // tm),
        in_specs=[pl.BlockSpec((None, tm, fw), row), pl.BlockSpec((None, tm, lw), row),
                  _layer_block(w_out, layer), pl.BlockSpec((None, tm, d), row),
                  _mod_spec(mods, layer, 2, d, row_of),
                  _layer_block(gains, layer, 1), _layer_block(gains, layer, 2),
                  _mod_spec(mods, layer, 3, d, row_of), _mod_spec(mods, layer, 4, d, row_of)],
        out_specs=[pl.BlockSpec((None, tm, d), row), pl.BlockSpec((None, tm, d), row)],
        out_shape=[jax.ShapeDtypeStruct((b, l, d), F32), jax.ShapeDtypeStruct((b, l, d), BF16)],
        compiler_params=_params("arbitrary", "arbitrary"),
        name="out_proj",
    )(four, rec, w_out, x, mods, gains, gains, mods, mods)


def _ffn_kernel(h_ref, wg_ref, wu_ref, wo_ref, x_ref, gate_ref, gain_ref, o_ref):
    j = pl.program_id(2)
    tm = o_ref.shape[0]
    sub = min(tm, FFN_SUB_ROWS)

    def chunk(first):
        for s in range(tm // sub):
            rows = slice(s * sub, (s + 1) * sub)
            h = h_ref[rows, :]
            g = _dot(h, wg_ref[...])
            u = _dot(h, wu_ref[...])
            act = (g * jax.nn.sigmoid(g) * u).astype(BF16)
            part = _dot(act, wo_ref[...])
            o_ref[rows, :] = part if first else o_ref[rows, :] + part

    pl.when(j == 0)(functools.partial(chunk, True))
    pl.when(j > 0)(functools.partial(chunk, False))

    @pl.when(j == pl.num_programs(2) - 1)
    def _():
        o_ref[...] = x_ref[...] + gate_ref[...] * _rms(o_ref[...], gain_ref[...])


def _ffn(h, w_ffn_in, w_ffn_out, x, mods, row_of, gains, layer):
    b, l, d = x.shape
    dff = w_ffn_out.shape[1]
    tm = _tile(l, FFN_ROW_TILE)
    tf = _tile(dff, FFN_COL_TILE)
    nj = dff // tf
    row = lambda bi, i, j: (bi, i, 0)
    return pl.pallas_call(
        _ffn_kernel,
        grid=(b, l // tm, nj),
        in_specs=[pl.BlockSpec((None, tm, d), row),
                  pl.BlockSpec((None, d, tf), lambda bi, i, j: (layer, 0, j)),
                  pl.BlockSpec((None, d, tf), lambda bi, i, j: (layer, 0, nj + j)),
                  pl.BlockSpec((None, tf, d), lambda bi, i, j: (layer, j, 0)),
                  pl.BlockSpec((None, tm, d), row, pipeline_mode=pl.Buffered(1)),
                  _mod_spec(mods, layer, 5, d, row_of),
                  _layer_block(gains, layer, 3)],
        out_specs=pl.BlockSpec((None, tm, d), row),
        out_shape=jax.ShapeDtypeStruct((b, l, d), F32),
        compiler_params=_params("arbitrary", "arbitrary", "arbitrary", vmem=60 * 1024 * 1024),
        name="ffn",
    )(h, w_ffn_in, w_ffn_in, w_ffn_out, x, mods, gains)


def kernel(x, c, ctx, c_ctx, w_ada, b_ada, norm_g, w_in, w_four, conv_w, conv_b, lru_w, lru_b, lru_lam,
           w_out, w_ffn_in, w_ffn_out):
    depth = w_ada.shape[0]
    bsz, _, d = x.shape
    lc = ctx.shape[1]
    fw = w_four.shape[1] * w_four.shape[2]
    lw = conv_w.shape[-1]

    pad = (-(bsz + 1)) % SUBLANES
    cs = jnp.concatenate([c, c_ctx[None, :], jnp.zeros((pad, d), F32)], axis=0)
    mods = _ada(cs, w_ada, b_ada)
    mods = mods.reshape(depth, mods.shape[1], 1, N_MOD * d)
    lat_row = lambda bi: bi
    ctx_row = lambda bi: bsz

    gains = norm_g.reshape(depth, norm_g.shape[1], 1, d)
    w_in_b = w_in.astype(BF16)
    w_out_b = w_out.astype(BF16)
    w_ffn_in_b = w_ffn_in.astype(BF16)
    w_ffn_out_b = w_ffn_out.astype(BF16)
    w_four_b = w_four.astype(BF16)
    w_gate_b = jnp.concatenate([lru_w[:, :, 0], lru_w[:, :, 1]], axis=-1).astype(BF16)
    scan_params = (conv_w, conv_b.reshape(depth, 1, lw), w_gate_b, lru_b,
                   lru_lam.reshape(depth, 2, 1, lw))
    zero_state = jnp.zeros((bsz, SUBLANES, lw), F32)

    ctx = ctx.reshape(1, bsz * lc, d)
    per_seq = lambda a: a.reshape(bsz, lc, a.shape[-1])
    flat = lambda a: a.reshape(1, bsz * lc, a.shape[-1])

    for layer in range(depth):
        last = layer == depth - 1
        f_lat, r_lat, g_lat = _inproj(x, mods, lat_row, gains, w_in_b, layer, fw, lw)
        f_ctx, r_ctx, g_ctx = _inproj(ctx, mods, ctx_row, gains, w_in_b, layer, fw, lw)

        recf_ctx, u_ctx, hf = _rglru(per_seq(r_ctx), layer, 0, scan_params, zero_state)
        if last:
            _, hb = _rglru(u_ctx, layer, 1, scan_params, zero_state)
        else:
            rec_ctx, hb = _rglru(u_ctx, layer, 1, scan_params, zero_state, recf_ctx, per_seq(g_ctx))
        recf_lat, u_lat, _ = _rglru(r_lat, layer, 0, scan_params, hf)
        rec_lat, _ = _rglru(u_lat, layer, 1, scan_params, hb, recf_lat, g_lat)

        four_lat = _fourier(f_lat, w_four_b, layer, GRID_W)
        x, h_lat = _outproj(four_lat, rec_lat, w_out_b, x, mods, lat_row, gains, layer)
        x = _ffn(h_lat, w_ffn_in_b, w_ffn_out_b, x, mods, lat_row, gains, layer)
        if not last:
            four_ctx = _fourier(per_seq(f_ctx), w_four_b, layer, None)
            ctx, h_ctx = _outproj(flat(four_ctx), flat(rec_ctx), w_out_b, ctx, mods, ctx_row, gains, layer)
            ctx = _ffn(h_ctx, w_ffn_in_b, w_ffn_out_b, ctx, mods, ctx_row, gains, layer)
    return x
```

```python
import functools
import math

import numpy as np
import jax
import jax.numpy as jnp
from jax import lax
from jax.experimental import pallas as pl
from jax.experimental.pallas import tpu as pltpu

GRID_W = 64
CONV_LEFT = 2
LRU_C = 8.0
EPS = 1e-6
N_MOD = 6

F32 = jnp.float32
BF16 = jnp.bfloat16

V7X_VMEM_LIMIT_BYTES = 56 * 1024 * 1024
SUBLANES = 8
LANES = 128
ROW_PITCH_PAD = 8
FOURIER_COL_UNROLL = 8
FOURIER_Q_BATCH = 4
ROW_TILE = 512
PROJ_SUB_ROWS = 256
FFN_ROW_TILE = 1024
FFN_SUB_ROWS = 512
FFN_COL_TILE = 512
FFN_OUT_COLS = 512
SCAN_CHUNK = 512
SCAN_UNROLL = 8
SQRT_GUARD = 1e-30


def _params(*sem, vmem=V7X_VMEM_LIMIT_BYTES):
    return pltpu.CompilerParams(dimension_semantics=sem, vmem_limit_bytes=vmem)


def _dot(a, b):
    return jnp.dot(a, b, preferred_element_type=F32)


def _tile(n, pref):
    t = min(n, pref)
    while n % t:
        t //= 2
    return t


def _rms(x, gain):
    ms = jnp.mean(x * x, axis=-1, keepdims=True)
    return x * lax.rsqrt(ms + EPS) * gain


def _resident(shape):
    nd = len(shape)
    return pl.BlockSpec(shape, lambda *_: (0,) * nd, pipeline_mode=pl.Buffered(1))


def _layer_block(arr, *lead):
    nlead = len(lead)
    rest = arr.shape[nlead:]
    idx = tuple(lead) + (0,) * len(rest)
    return pl.BlockSpec((None,) * nlead + tuple(rest), lambda *_: idx, pipeline_mode=pl.Buffered(1))


def _mod_spec(mods, layer, k, d, row_of):
    return pl.BlockSpec((None, None, 1, d), lambda bi, *_: (layer, row_of(bi), 0, k))


def _ada_kernel(c_ref, w_ref, b_ref, o_ref):
    c = c_ref[...]
    s = (c * jax.nn.sigmoid(c)).astype(BF16)
    o_ref[...] = _dot(s, w_ref[...].astype(BF16)) + b_ref[...]


def _ada(cs, w_ada, b_ada):
    depth, d, nm = w_ada.shape
    rows = cs.shape[0]
    tn = _tile(nm, 1024)
    return pl.pallas_call(
        _ada_kernel,
        grid=(depth, nm // tn),
        in_specs=[
            pl.BlockSpec((rows, d), lambda l, j: (0, 0)),
            pl.BlockSpec((None, d, tn), lambda l, j: (l, 0, j)),
            pl.BlockSpec((None, 1, tn), lambda l, j: (l, 0, j)),
        ],
        out_specs=pl.BlockSpec((None, rows, tn), lambda l, j: (l, 0, j)),
        out_shape=jax.ShapeDtypeStruct((depth, rows, nm), F32),
        compiler_params=_params("arbitrary", "arbitrary"),
        name="ada_mod",
    )(cs, w_ada, b_ada.reshape(depth, 1, nm))


def _inproj_kernel(x_ref, sh_ref, sc_ref, gain_ref, w_ref, f_ref, rt_ref, g_ref):
    tm = x_ref.shape[0]
    nf = f_ref.shape[-1]
    nr = g_ref.shape[-1]
    sub = min(tm, PROJ_SUB_ROWS)
    for s in range(tm // sub):
        rows = slice(s * sub, (s + 1) * sub)
        h = _rms(x_ref[rows, :], gain_ref[...])
        h = (h * (1.0 + sc_ref[...]) + sh_ref[...]).astype(BF16)
        g_ref[rows, :] = jax.nn.gelu(_dot(h, w_ref[:, nf + nr:])).astype(g_ref.dtype)
        r = _dot(h, w_ref[:, nf:nf + nr])
        for j in range(nr // LANES):
            rt_ref[pl.ds(s * sub * SUBLANES + j, sub, stride=SUBLANES), :] = r[:, j * LANES:(j + 1) * LANES]
        f_ref[rows, :] = _dot(h, w_ref[:, :nf]).astype(f_ref.dtype)


def _inproj(x, mods, row_of, gains, w_in, layer, fw, lw):
    b, l, d = x.shape
    assert lw == SUBLANES * LANES, "time-major layout packs the recurrent width into one (8, 128) tile"
    tm = _tile(l, ROW_TILE)
    row = lambda bi, i: (bi, i, 0)
    return pl.pallas_call(
        _inproj_kernel,
        grid=(b, l // tm),
        in_specs=[pl.BlockSpec((None, tm, d), row),
                  _mod_spec(mods, layer, 0, d, row_of), _mod_spec(mods, layer, 1, d, row_of),
                  _layer_block(gains, layer, 0), _layer_block(w_in, layer)],
        out_specs=[pl.BlockSpec((None, tm, fw), row), pl.BlockSpec((None, tm * SUBLANES, LANES), row),
                   pl.BlockSpec((None, tm, lw), row)],
        out_shape=[jax.ShapeDtypeStruct((b, l, fw), BF16),
                   jax.ShapeDtypeStruct((b, l * SUBLANES, LANES), F32),
                   jax.ShapeDtypeStruct((b, l, lw), BF16)],
        compiler_params=_params("arbitrary", "arbitrary"),
        name="in_proj",
    )(x, mods, mods, gains, w_in)


def _dft(n):
    k = np.arange(n)
    ang = 2.0 * np.pi * ((k[:, None] * k[None, :]) % n) / n
    return np.cos(ang) / math.sqrt(n), np.sin(ang) / math.sqrt(n)


def _fold_channel_dft(ck_ref, sk_ref, wf_ref):
    wf = wf_ref[...]
    return _dot(ck_ref[...], wf).astype(BF16), (-_dot(sk_ref[...], wf)).astype(BF16)


def _fourier_grid_kernel(u_ref, mc_ref, mr_ref, ck_ref, sk_ref, wf_ref, o_ref, pc_ref, ps_ref, *,
                         rows, cols, pitch):
    gd = u_ref.shape[-1]
    nslab = gd // LANES
    wa, wb = _fold_channel_dft(ck_ref, sk_ref, wf_ref)
    wab = jnp.concatenate([wa, wb], axis=0)
    mc = mc_ref[...]
    mr = mr_ref[...]

    def col_stage(r, carry):
        src = pl.multiple_of(r * cols, cols)
        dst = pl.multiple_of(r * pitch, SUBLANES)
        t = _dot(mc, u_ref[pl.ds(src, cols), :])
        for s in range(nslab):
            lanes = slice(s * LANES, (s + 1) * LANES)
            pc_ref[s, pl.ds(dst, cols), :] = t[:cols, lanes]
            ps_ref[s, pl.ds(dst, cols), :] = t[cols:, lanes]
        return carry

    lax.fori_loop(0, rows, col_stage, 0, unroll=FOURIER_COL_UNROLL)

    def gather(ref, q):
        return jnp.concatenate([ref.at[s][pl.ds(q, rows, stride=pitch), :] for s in range(nslab)],
                               axis=-1).astype(BF16)

    def row_stage(it, carry):
        q0 = it * FOURIER_Q_BATCH
        x = jnp.concatenate(
            [jnp.concatenate([gather(pc_ref, q0 + k), gather(ps_ref, q0 + k)], axis=0)
             for k in range(FOURIER_Q_BATCH)], axis=-1)
        y = _dot(mr, x).astype(BF16)
        lhs = jnp.concatenate(
            [jnp.concatenate([y[:rows, k * gd:(k + 1) * gd], y[rows:, k * gd:(k + 1) * gd]], axis=-1)
             for k in range(FOURIER_Q_BATCH)], axis=0)
        z = _dot(lhs, wab)
        for k in range(FOURIER_Q_BATCH):
            for s in range(nslab):
                pc_ref.at[s][pl.ds(q0 + k, rows, stride=pitch), :] = (
                    z[k * rows:(k + 1) * rows, s * LANES:(s + 1) * LANES])
        return carry

    lax.fori_loop(0, cols // FOURIER_Q_BATCH, row_stage, 0)

    def emit(r, carry):
        src = pl.multiple_of(r * pitch, SUBLANES)
        dst = pl.multiple_of(r * cols, cols)
        for s in range(nslab):
            o_ref[pl.ds(dst, cols), s * LANES:(s + 1) * LANES] = (
                pc_ref[s, pl.ds(src, cols), :].astype(o_ref.dtype))
        return carry

    lax.fori_loop(0, rows, emit, 0, unroll=FOURIER_COL_UNROLL)


def _fourier_seq_kernel(u_ref, ml_ref, ck_ref, sk_ref, wf_ref, o_ref):
    l = u_ref.shape[0]
    wa, wb = _fold_channel_dft(ck_ref, sk_ref, wf_ref)
    y = _dot(ml_ref[...], u_ref[...])
    z = _dot(y[:l].astype(BF16), wa) + _dot(y[l:].astype(BF16), wb)
    o_ref[...] = z.astype(o_ref.dtype)


def _fourier(u, w_four, layer, grid_w):
    b, l, fw = u.shape
    _, g, gd, _ = w_four.shape
    ck, sk = _dft(gd)
    ck, sk = jnp.asarray(ck, BF16), jnp.asarray(sk, BF16)
    blk = pl.BlockSpec((None, l, gd), lambda bi, gi: (bi, 0, gi))
    wspec = pl.BlockSpec((None, None, gd, gd), lambda bi, gi: (layer, gi, 0, 0))
    common = dict(
        grid=(b, g),
        out_specs=blk,
        out_shape=jax.ShapeDtypeStruct((b, l, fw), BF16),
        compiler_params=_params("arbitrary", "arbitrary"),
    )
    if grid_w is None:
        cl, sl = _dft(l)
        ml = jnp.asarray(np.concatenate([cl, sl], axis=0), BF16)
        return pl.pallas_call(
            _fourier_seq_kernel,
            in_specs=[blk, _resident(ml.shape), _resident(ck.shape), _resident(sk.shape), wspec],
            name="fourier_seq", **common,
        )(u, ml, ck, sk, w_four)
    cols = grid_w
    rows = l // cols
    pitch = cols + ROW_PITCH_PAD
    cc, sc = _dft(cols)
    cr, sr = _dft(rows)
    mc = jnp.asarray(np.concatenate([cc, sc], axis=0), BF16)
    mr = jnp.asarray(np.block([[cr, -sr], [sr, cr]]), BF16)
    scratch = pltpu.VMEM((gd // LANES, rows * pitch, LANES), F32)
    return pl.pallas_call(
        functools.partial(_fourier_grid_kernel, rows=rows, cols=cols, pitch=pitch),
        in_specs=[blk, _resident(mc.shape), _resident(mr.shape), _resident(ck.shape),
                  _resident(sk.shape), wspec],
        scratch_shapes=[scratch, scratch],
        name="fourier_grid", **common,
    )(u, mc, mr, ck, sk, w_four)


def _to_time_major(dst, x):
    t = x.shape[0]
    for j in range(x.shape[1] // LANES):
        dst[pl.ds(j, t, stride=SUBLANES), :] = x[:, j * LANES:(j + 1) * LANES]


def _from_time_major(src, t, j):
    return src[pl.ds(j, t, stride=SUBLANES), :]


def _rglru_kernel(*refs, reverse, conv, gated, heads):
    refs = list(refs)
    if conv:
        rt_ref, rp_ref, rn_ref, cw_ref, cb_ref = refs[:5]
        del refs[:5]
    else:
        u_ref = refs.pop(0)
    wg_ref, bg_ref, lam_ref, h0_ref = refs[:4]
    del refs[:4]
    if gated:
        other_ref, g_ref = refs[:2]
        del refs[:2]
    o_ref = refs.pop(0)
    if conv:
        uo_ref = refs.pop(0)
    hfin_ref = refs.pop(0)
    if conv:
        ut_sc = refs.pop(0)
    at_sc, bt_sc, ht_sc, carry = refs

    nbat, t, w = o_ref.shape
    hd = w // heads
    ncol = w // LANES
    i = pl.program_id(1)
    n = pl.num_programs(1)
    ci = (n - 1 - i) if reverse else i

    @pl.when(i == 0)
    def _():
        carry[...] = h0_ref[...]

    half_clam = (-0.5 * LRU_C) * jax.nn.softplus(-lam_ref[...])

    for bi in range(nbat):
        if conv:
            main = rt_ref[bi]
            prev = jnp.where(ci == 0, 0.0, rp_ref[bi])
            nxt = jnp.where(ci == n - 1, 0.0, rn_ref[bi])
            ext = jnp.concatenate([prev, main, nxt], axis=0)
            ut = cb_ref[...][None]
            for k in range(cw_ref.shape[0]):
                tap = ext[k * SUBLANES:(k + t) * SUBLANES]
                ut = ut + cw_ref[k][None] * tap.reshape(t, SUBLANES, LANES)
            ut_sc[bi] = ut.reshape(t * SUBLANES, LANES)
            u = jnp.concatenate([_from_time_major(ut_sc.at[bi], t, j) for j in range(ncol)], axis=-1)
            uo_ref[bi] = u
        else:
            u = u_ref[bi]
        ub = u.astype(BF16)

        for h in range(heads):
            ch = slice(h * hd, (h + 1) * hd)
            gates = _dot(ub[:, ch], wg_ref[h])
            tr = jnp.tanh(gates[:, :hd] + bg_ref[0:1, ch])
            ti = jnp.tanh(gates[:, hd:] + bg_ref[1:2, ch])
            a = jnp.exp(half_clam[:, ch] + half_clam[:, ch] * tr)
            m = jnp.maximum((1.0 - a) * (1.0 + a), 0.0)
            bx = (m * lax.rsqrt(jnp.maximum(m, SQRT_GUARD))) * ((0.5 + 0.5 * ti) * u[:, ch])
            for jj in range(hd // LANES):
                j = h * (hd // LANES) + jj
                lanes = slice(jj * LANES, (jj + 1) * LANES)
                at_sc.at[bi][pl.ds(j, t, stride=SUBLANES), :] = a[:, lanes]
                bt_sc.at[bi][pl.ds(j, t, stride=SUBLANES), :] = bx[:, lanes]

    def step(s, hs):
        tpos = (t - 1 - s) if reverse else s
        base = pl.multiple_of(tpos * SUBLANES, SUBLANES)
        out = []
        for bi in range(nbat):
            hh = at_sc[bi, pl.ds(base, SUBLANES), :] * hs[bi] + bt_sc[bi, pl.ds(base, SUBLANES), :]
            ht_sc[bi, pl.ds(base, SUBLANES), :] = hh
            out.append(hh)
        return tuple(out)

    hfin = lax.fori_loop(0, t, step, tuple(carry[bi] for bi in range(nbat)), unroll=SCAN_UNROLL)
    for bi in range(nbat):
        carry[bi] = hfin[bi]
        hfin_ref[bi] = hfin[bi]
        for j in range(ncol):
            lanes = slice(j * LANES, (j + 1) * LANES)
            rec = _from_time_major(ht_sc.at[bi], t, j)
            if gated:
                o_ref[bi, :, lanes] = (g_ref[bi, :, lanes].astype(F32)
                                       * (other_ref[bi, :, lanes] + rec)).astype(o_ref.dtype)
            else:
                o_ref[bi, :, lanes] = rec


def _rglru(src, layer, direction, params, h0, other=None, gate=None):
    conv_w, conv_b, w_gate, b_gate, lam = params
    reverse = bool(direction)
    conv = not reverse
    gated = other is not None
    heads = w_gate.shape[2]
    b = src.shape[0]
    if conv:
        l, w = src.shape[1] // SUBLANES, SUBLANES * LANES
    else:
        l, w = src.shape[1:]
    nbat = 2 if b % 2 == 0 else 1
    t = _tile(l, SCAN_CHUNK)
    n = l // t

    def pos(i):
        return (n - 1 - i) if reverse else i

    main = pl.BlockSpec((nbat, t, w), lambda bi, i: (bi, pos(i), 0))
    state = pl.BlockSpec((nbat, SUBLANES, LANES), lambda bi, i: (bi, 0, 0))
    if conv:
        in_specs = [
            pl.BlockSpec((nbat, t * SUBLANES, LANES), lambda bi, i: (bi, pos(i), 0)),
            pl.BlockSpec((nbat, 2 * SUBLANES, LANES),
                         lambda bi, i: (bi, jnp.maximum(pos(i) * (t // 2) - 1, 0), 0)),
            pl.BlockSpec((nbat, SUBLANES, LANES),
                         lambda bi, i: (bi, jnp.minimum((pos(i) + 1) * t, l - 1), 0)),
            _layer_block(conv_w, layer), _layer_block(conv_b, layer)]
        args = [src, src, src, conv_w, conv_b]
    else:
        in_specs, args = [main], [src]
    in_specs += [_layer_block(w_gate, layer, direction), _layer_block(b_gate, layer, direction),
                 _layer_block(lam, layer, direction), state]
    args += [w_gate, b_gate, lam, h0]
    if gated:
        in_specs += [main, main]
        args += [other, gate]
    out_specs = [main] + ([main] if conv else []) + [state]
    out_shape = ([jax.ShapeDtypeStruct((b, l, w), BF16 if gated else F32)]
                 + ([jax.ShapeDtypeStruct((b, l, w), F32)] if conv else [])
                 + [jax.ShapeDtypeStruct((b, SUBLANES, LANES), F32)])
    tmaj = pltpu.VMEM((nbat, t * SUBLANES, LANES), F32)
    scratch = ([tmaj] if conv else []) + [tmaj, tmaj, tmaj, pltpu.VMEM((nbat, SUBLANES, LANES), F32)]
    return pl.pallas_call(
        functools.partial(_rglru_kernel, reverse=reverse, conv=conv, gated=gated, heads=heads),
        grid=(b // nbat, n),
        in_specs=in_specs,
        out_specs=out_specs,
        out_shape=out_shape,
        scratch_shapes=scratch,
        compiler_params=_params("arbitrary", "arbitrary"),
        name="rglru_bwd" if reverse else "rglru_fwd",
    )(*args)


def _outproj_kernel(four_ref, rec_ref, w_ref, x_ref, gate_ref, g_post_ref, g_pre_ref, sh_ref, sc_ref,
                    xo_ref, h_ref):
    tm = x_ref.shape[0]
    sub = min(tm, PROJ_SUB_ROWS)
    for s in range(tm // sub):
        rows = slice(s * sub, (s + 1) * sub)
        lhs = jnp.concatenate([four_ref[rows, :], rec_ref[rows, :]], axis=-1)
        y = _dot(lhs, w_ref[...])
        xn = x_ref[rows, :] + gate_ref[...] * _rms(y, g_post_ref[...])
        xo_ref[rows, :] = xn
        h = _rms(xn, g_pre_ref[...])
        h_ref[rows, :] = (h * (1.0 + sc_ref[...]) + sh_ref[...]).astype(h_ref.dtype)


def _outproj(four, rec, w_out, x, mods, row_of, gains, layer):
    b, l, d = x.shape
    fw, lw = four.shape[-1], rec.shape[-1]
    tm = _tile(l, ROW_TILE)
    row = lambda bi, i: (bi, i, 0)
    return pl.pallas_call(
        _outproj_kernel,
        grid=(b, l // tm),
        in_specs=[pl.BlockSpec((None, tm, fw), row), pl.BlockSpec((None, tm, lw), row),
                  _layer_block(w_out, layer), pl.BlockSpec((None, tm, d), row),
                  _mod_spec(mods, layer, 2, d, row_of),
                  _layer_block(gains, layer, 1), _layer_block(gains, layer, 2),
                  _mod_spec(mods, layer, 3, d, row_of), _mod_spec(mods, layer, 4, d, row_of)],
        out_specs=[pl.BlockSpec((None, tm, d), row), pl.BlockSpec((None, tm, d), row)],
        out_shape=[jax.ShapeDtypeStruct((b, l, d), F32), jax.ShapeDtypeStruct((b, l, d), BF16)],
        compiler_params=_params("arbitrary", "arbitrary"),
        name="out_proj",
    )(four, rec, w_out, x, mods, gains, gains, mods, mods)


def _ffn_kernel(h_ref, wg_ref, wu_ref, wo_ref, x_ref, gate_ref, gain_ref, o_ref):
    j = pl.program_id(2)
    tm, d = o_ref.shape
    sub = min(tm, FFN_SUB_ROWS)

    def chunk(first):
        for s in range(tm // sub):
            rows = slice(s * sub, (s + 1) * sub)
            h = h_ref[rows, :]
            g = _dot(h, wg_ref[...])
            u = _dot(h, wu_ref[...])
            act = (g * jax.nn.sigmoid(g) * u).astype(BF16)
            for c in range(0, d, FFN_OUT_COLS):
                cols = slice(c, c + FFN_OUT_COLS)
                part = _dot(act, wo_ref[:, cols])
                o_ref[rows, cols] = part if first else o_ref[rows, cols] + part

    pl.when(j == 0)(functools.partial(chunk, True))
    pl.when(j > 0)(functools.partial(chunk, False))

    @pl.when(j == pl.num_programs(2) - 1)
    def _():
        o_ref[...] = x_ref[...] + gate_ref[...] * _rms(o_ref[...], gain_ref[...])


def _ffn(h, w_ffn_in, w_ffn_out, x, mods, row_of, gains, layer):
    b, l, d = x.shape
    dff = w_ffn_out.shape[1]
    tm = _tile(l, FFN_ROW_TILE)
    tf = _tile(dff, FFN_COL_TILE)
    nj = dff // tf
    row = lambda bi, i, j: (bi, i, 0)
    return pl.pallas_call(
        _ffn_kernel,
        grid=(b, l // tm, nj),
        in_specs=[pl.BlockSpec((None, tm, d), row),
                  pl.BlockSpec((None, d, tf), lambda bi, i, j: (layer, 0, j)),
                  pl.BlockSpec((None, d, tf), lambda bi, i, j: (layer, 0, nj + j)),
                  pl.BlockSpec((None, tf, d), lambda bi, i, j: (layer, j, 0)),
                  pl.BlockSpec((None, tm, d), row),
                  _mod_spec(mods, layer, 5, d, row_of),
                  _layer_block(gains, layer, 3)],
        out_specs=pl.BlockSpec((None, tm, d), row),
        out_shape=jax.ShapeDtypeStruct((b, l, d), F32),
        compiler_params=_params("arbitrary", "arbitrary", "arbitrary", vmem=60 * 1024 * 1024),
        name="ffn",
    )(h, w_ffn_in, w_ffn_in, w_ffn_out, x, mods, gains)


def kernel(x, c, ctx, c_ctx, w_ada, b_ada, norm_g, w_in, w_four, conv_w, conv_b, lru_w, lru_b, lru_lam,
           w_out, w_ffn_in, w_ffn_out):
    depth = w_ada.shape[0]
    bsz, _, d = x.shape
    lc = ctx.shape[1]
    fw = w_four.shape[1] * w_four.shape[2]
    lw = conv_w.shape[-1]

    pad = (-(bsz + 1)) % SUBLANES
    cs = jnp.concatenate([c, c_ctx[None, :], jnp.zeros((pad, d), F32)], axis=0)
    mods = _ada(cs, w_ada, b_ada)
    mods = mods.reshape(depth, mods.shape[1], 1, N_MOD * d)
    lat_row = lambda bi: bi
    ctx_row = lambda bi: bsz

    gains = norm_g.reshape(depth, norm_g.shape[1], 1, d)
    w_in_b = w_in.astype(BF16)
    w_out_b = w_out.astype(BF16)
    w_ffn_in_b = w_ffn_in.astype(BF16)
    w_ffn_out_b = w_ffn_out.astype(BF16)
    w_four_b = w_four.astype(BF16)
    w_gate_b = (0.5 * jnp.concatenate([lru_w[:, :, 0], lru_w[:, :, 1]], axis=-1)).astype(BF16)
    scan_params = (conv_w.reshape(depth, conv_w.shape[1], SUBLANES, LANES),
                   conv_b.reshape(depth, SUBLANES, LANES), w_gate_b, 0.5 * lru_b,
                   lru_lam.reshape(depth, 2, 1, lw))
    zero_state = jnp.zeros((bsz, SUBLANES, LANES), F32)

    ctx = ctx.reshape(1, bsz * lc, d)
    per_seq = lambda a: a.reshape(bsz, -1, a.shape[-1])
    flat = lambda a: a.reshape(1, bsz * lc, a.shape[-1])

    for layer in range(depth):
        last = layer == depth - 1
        f_lat, rt_lat, g_lat = _inproj(x, mods, lat_row, gains, w_in_b, layer, fw, lw)
        f_ctx, rt_ctx, g_ctx = _inproj(ctx, mods, ctx_row, gains, w_in_b, layer, fw, lw)

        recf_ctx, u_ctx, hf = _rglru(per_seq(rt_ctx), layer, 0, scan_params, zero_state)
        if last:
            _, hb = _rglru(u_ctx, layer, 1, scan_params, zero_state)
        else:
            rec_ctx, hb = _rglru(u_ctx, layer, 1, scan_params, zero_state, recf_ctx, per_seq(g_ctx))
        recf_lat, u_lat, _ = _rglru(rt_lat, layer, 0, scan_params, hf)
        rec_lat, _ = _rglru(u_lat, layer, 1, scan_params, hb, recf_lat, g_lat)

        four_lat = _fourier(f_lat, w_four_b, layer, GRID_W)
        x, h_lat = _outproj(four_lat, rec_lat, w_out_b, x, mods, lat_row, gains, layer)
        x = _ffn(h_lat, w_ffn_in_b, w_ffn_out_b, x, mods, lat_row, gains, layer)
        if not last:
            four_ctx = _fourier(per_seq(f_ctx), w_four_b, layer, None)
            ctx, h_ctx = _outproj(flat(four_ctx), flat(rec_ctx), w_out_b, ctx, mods, ctx_row, gains, layer)
            ctx = _ffn(h_ctx, w_ffn_in_b, w_ffn_out_b, ctx, mods, ctx_row, gains, layer)
    return x
```

```python
import functools
import math

import numpy as np
import jax
import jax.numpy as jnp
from jax import lax
from jax.experimental import pallas as pl
from jax.experimental.pallas import tpu as pltpu

GRID_W = 64
CONV_LEFT = 2
LRU_C = 8.0
EPS = 1e-6
N_MOD = 6

F32 = jnp.float32
BF16 = jnp.bfloat16

V7X_VMEM_LIMIT_BYTES = 56 * 1024 * 1024
SUBLANES = 8
LANES = 128
ROW_PITCH_PAD = 8
FOURIER_COL_UNROLL = 8
FOURIER_Q_BATCH = 4
IN_ROW_TILE = 1024
ROW_TILE = 512
PROJ_SUB_ROWS = 256
FFN_ROW_TILE = 1024
FFN_SUB_ROWS = 512
FFN_COL_TILE = 512
FFN_OUT_COLS = 512
SCAN_CHUNK = 512
SCAN_UNROLL = 8
SQRT_GUARD = 1e-30


def _params(*sem, vmem=V7X_VMEM_LIMIT_BYTES):
    return pltpu.CompilerParams(dimension_semantics=sem, vmem_limit_bytes=vmem)


def _dot(a, b):
    return jnp.dot(a, b, preferred_element_type=F32)


def _tile(n, pref):
    t = min(n, pref)
    while n % t:
        t //= 2
    return t


def _rms(x, gain):
    ms = jnp.mean(x * x, axis=-1, keepdims=True)
    return x * lax.rsqrt(ms + EPS) * gain


def _resident(shape):
    nd = len(shape)
    return pl.BlockSpec(shape, lambda *_: (0,) * nd, pipeline_mode=pl.Buffered(1))


def _layer_block(arr, *lead):
    nlead = len(lead)
    rest = arr.shape[nlead:]
    idx = tuple(lead) + (0,) * len(rest)
    return pl.BlockSpec((None,) * nlead + tuple(rest), lambda *_: idx, pipeline_mode=pl.Buffered(1))


def _mod_spec(mods, layer, row_of):
    return pl.BlockSpec((None, None, 1, mods.shape[-1]), lambda bi, *_: (layer, row_of(bi), 0, 0))


def _mod(mod_ref, k):
    d = mod_ref.shape[-1] // N_MOD
    return mod_ref[:, k * d:(k + 1) * d]


def _ada_kernel(c_ref, w_ref, b_ref, o_ref):
    c = c_ref[...]
    s = (c * jax.nn.sigmoid(c)).astype(BF16)
    o_ref[...] = _dot(s, w_ref[...].astype(BF16)) + b_ref[...]


def _ada(cs, w_ada, b_ada):
    depth, d, nm = w_ada.shape
    rows = cs.shape[0]
    tn = _tile(nm, 1024)
    return pl.pallas_call(
        _ada_kernel,
        grid=(depth, nm // tn),
        in_specs=[
            pl.BlockSpec((rows, d), lambda l, j: (0, 0)),
            pl.BlockSpec((None, d, tn), lambda l, j: (l, 0, j)),
            pl.BlockSpec((None, 1, tn), lambda l, j: (l, 0, j)),
        ],
        out_specs=pl.BlockSpec((None, rows, tn), lambda l, j: (l, 0, j)),
        out_shape=jax.ShapeDtypeStruct((depth, rows, nm), F32),
        compiler_params=_params("arbitrary", "arbitrary"),
        name="ada_mod",
    )(cs, w_ada, b_ada.reshape(depth, 1, nm))


def _inproj_kernel(x_ref, mod_ref, gain_ref, w_ref, f_ref, rt_ref, g_ref):
    tm = x_ref.shape[0]
    nf = f_ref.shape[-1]
    nr = g_ref.shape[-1]
    sub = min(tm, PROJ_SUB_ROWS)
    gain = gain_ref[0] * (1.0 + _mod(mod_ref, 1))
    shift = _mod(mod_ref, 0)
    for s in range(tm // sub):
        rows = slice(s * sub, (s + 1) * sub)
        h = (_rms(x_ref[rows, :], gain) + shift).astype(BF16)
        g_ref[rows, :] = jax.nn.gelu(_dot(h, w_ref[:, nf + nr:])).astype(g_ref.dtype)
        r = _dot(h, w_ref[:, nf:nf + nr])
        for j in range(nr // LANES):
            rt_ref[pl.ds(s * sub * SUBLANES + j, sub, stride=SUBLANES), :] = r[:, j * LANES:(j + 1) * LANES]
        f_ref[rows, :] = _dot(h, w_ref[:, :nf]).astype(f_ref.dtype)


def _inproj(x, mods, row_of, gains, w_in, layer, fw, lw):
    b, l, d = x.shape
    assert lw == SUBLANES * LANES, "time-major layout packs the recurrent width into one (8, 128) tile"
    tm = _tile(l, IN_ROW_TILE)
    row = lambda bi, i: (bi, i, 0)
    return pl.pallas_call(
        _inproj_kernel,
        grid=(b, l // tm),
        in_specs=[pl.BlockSpec((None, tm, d), row), _mod_spec(mods, layer, row_of),
                  _layer_block(gains, layer), _layer_block(w_in, layer)],
        out_specs=[pl.BlockSpec((None, tm, fw), row), pl.BlockSpec((None, tm * SUBLANES, LANES), row),
                   pl.BlockSpec((None, tm, lw), row)],
        out_shape=[jax.ShapeDtypeStruct((b, l, fw), BF16),
                   jax.ShapeDtypeStruct((b, l * SUBLANES, LANES), F32),
                   jax.ShapeDtypeStruct((b, l, lw), BF16)],
        compiler_params=_params("arbitrary", "arbitrary"),
        name="in_proj",
    )(x, mods, gains, w_in)


def _dft(n):
    k = np.arange(n)
    ang = 2.0 * np.pi * ((k[:, None] * k[None, :]) % n) / n
    return np.cos(ang) / math.sqrt(n), np.sin(ang) / math.sqrt(n)


def _fold_channel_dft(ck_ref, sk_ref, wf_ref):
    wf = wf_ref[...]
    return _dot(ck_ref[...], wf).astype(BF16), (-_dot(sk_ref[...], wf)).astype(BF16)


def _fourier_grid_kernel(u_ref, mc_ref, mr_ref, ck_ref, sk_ref, wf_ref, o_ref, pc_ref, ps_ref, *,
                         rows, cols, pitch):
    gd = u_ref.shape[-1]
    nslab = gd // LANES
    wa, wb = _fold_channel_dft(ck_ref, sk_ref, wf_ref)
    wab = jnp.concatenate([wa, wb], axis=0)
    mc = mc_ref[...]
    mr = mr_ref[...]

    def col_stage(r, carry):
        src = pl.multiple_of(r * cols, cols)
        dst = pl.multiple_of(r * pitch, SUBLANES)
        t = _dot(mc, u_ref[pl.ds(src, cols), :])
        for s in range(nslab):
            lanes = slice(s * LANES, (s + 1) * LANES)
            pc_ref[s, pl.ds(dst, cols), :] = t[:cols, lanes]
            ps_ref[s, pl.ds(dst, cols), :] = t[cols:, lanes]
        return carry

    lax.fori_loop(0, rows, col_stage, 0, unroll=FOURIER_COL_UNROLL)

    def gather(ref, q):
        return jnp.concatenate([ref.at[s][pl.ds(q, rows, stride=pitch), :] for s in range(nslab)],
                               axis=-1).astype(BF16)

    def row_stage(it, carry):
        q0 = it * FOURIER_Q_BATCH
        x = jnp.concatenate(
            [jnp.concatenate([gather(pc_ref, q0 + k), gather(ps_ref, q0 + k)], axis=0)
             for k in range(FOURIER_Q_BATCH)], axis=-1)
        y = _dot(mr, x).astype(BF16)
        lhs = jnp.concatenate(
            [jnp.concatenate([y[:rows, k * gd:(k + 1) * gd], y[rows:, k * gd:(k + 1) * gd]], axis=-1)
             for k in range(FOURIER_Q_BATCH)], axis=0)
        z = _dot(lhs, wab)
        for k in range(FOURIER_Q_BATCH):
            for s in range(nslab):
                pc_ref.at[s][pl.ds(q0 + k, rows, stride=pitch), :] = (
                    z[k * rows:(k + 1) * rows, s * LANES:(s + 1) * LANES])
        return carry

    lax.fori_loop(0, cols // FOURIER_Q_BATCH, row_stage, 0)

    def emit(r, carry):
        src = pl.multiple_of(r * pitch, SUBLANES)
        dst = pl.multiple_of(r * cols, cols)
        for s in range(nslab):
            o_ref[pl.ds(dst, cols), s * LANES:(s + 1) * LANES] = (
                pc_ref[s, pl.ds(src, cols), :].astype(o_ref.dtype))
        return carry

    lax.fori_loop(0, rows, emit, 0, unroll=FOURIER_COL_UNROLL)


def _fourier_seq_kernel(u_ref, ml_ref, ck_ref, sk_ref, wf_ref, o_ref):
    l = u_ref.shape[0]
    wa, wb = _fold_channel_dft(ck_ref, sk_ref, wf_ref)
    y = _dot(ml_ref[...], u_ref[...])
    z = _dot(y[:l].astype(BF16), wa) + _dot(y[l:].astype(BF16), wb)
    o_ref[...] = z.astype(o_ref.dtype)


def _fourier(u, w_four, layer, grid_w):
    b, l, fw = u.shape
    _, g, gd, _ = w_four.shape
    ck, sk = _dft(gd)
    ck, sk = jnp.asarray(ck, BF16), jnp.asarray(sk, BF16)
    blk = pl.BlockSpec((None, l, gd), lambda bi, gi: (bi, 0, gi))
    wspec = pl.BlockSpec((None, None, gd, gd), lambda bi, gi: (layer, gi, 0, 0))
    common = dict(
        grid=(b, g),
        out_specs=blk,
        out_shape=jax.ShapeDtypeStruct((b, l, fw), BF16),
        compiler_params=_params("arbitrary", "arbitrary"),
    )
    if grid_w is None:
        cl, sl = _dft(l)
        ml = jnp.asarray(np.concatenate([cl, sl], axis=0), BF16)
        return pl.pallas_call(
            _fourier_seq_kernel,
            in_specs=[blk, _resident(ml.shape), _resident(ck.shape), _resident(sk.shape), wspec],
            name="fourier_seq", **common,
        )(u, ml, ck, sk, w_four)
    cols = grid_w
    rows = l // cols
    pitch = cols + ROW_PITCH_PAD
    cc, sc = _dft(cols)
    cr, sr = _dft(rows)
    mc = jnp.asarray(np.concatenate([cc, sc], axis=0), BF16)
    mr = jnp.asarray(np.block([[cr, -sr], [sr, cr]]), BF16)
    scratch = pltpu.VMEM((gd // LANES, rows * pitch, LANES), F32)
    return pl.pallas_call(
        functools.partial(_fourier_grid_kernel, rows=rows, cols=cols, pitch=pitch),
        in_specs=[blk, _resident(mc.shape), _resident(mr.shape), _resident(ck.shape),
                  _resident(sk.shape), wspec],
        scratch_shapes=[scratch, scratch],
        name="fourier_grid", **common,
    )(u, mc, mr, ck, sk, w_four)


def _to_time_major(dst, x):
    t = x.shape[0]
    for j in range(x.shape[1] // LANES):
        dst[pl.ds(j, t, stride=SUBLANES), :] = x[:, j * LANES:(j + 1) * LANES]


def _from_time_major(src, t, j):
    return src[pl.ds(j, t, stride=SUBLANES), :]


def _rglru_kernel(*refs, reverse, conv, gated, heads):
    refs = list(refs)
    if conv:
        rt_ref, rp_ref, rn_ref, cw_ref, cb_ref = refs[:5]
        del refs[:5]
    else:
        u_ref = refs.pop(0)
    wg_ref, bg_ref, lam_ref, h0_ref = refs[:4]
    del refs[:4]
    if gated:
        other_ref, g_ref = refs[:2]
        del refs[:2]
    o_ref = refs.pop(0)
    if conv:
        uo_ref = refs.pop(0)
    hfin_ref = refs.pop(0)
    if conv:
        ut_sc = refs.pop(0)
    at_sc, bt_sc, ht_sc, carry = refs

    nbat, t, w = o_ref.shape
    hd = w // heads
    ncol = w // LANES
    i = pl.program_id(1)
    n = pl.num_programs(1)
    ci = (n - 1 - i) if reverse else i

    @pl.when(i == 0)
    def _():
        carry[...] = h0_ref[...]

    half_clam = (-0.5 * LRU_C) * jax.nn.softplus(-lam_ref[...])

    for bi in range(nbat):
        if conv:
            main = rt_ref[bi]
            prev = jnp.where(ci == 0, 0.0, rp_ref[bi])
            nxt = jnp.where(ci == n - 1, 0.0, rn_ref[bi])
            ext = jnp.concatenate([prev, main, nxt], axis=0)
            ut = cb_ref[...][None]
            for k in range(cw_ref.shape[0]):
                tap = ext[k * SUBLANES:(k + t) * SUBLANES]
                ut = ut + cw_ref[k][None] * tap.reshape(t, SUBLANES, LANES)
            ut_sc[bi] = ut.reshape(t * SUBLANES, LANES)
            u = jnp.concatenate([_from_time_major(ut_sc.at[bi], t, j) for j in range(ncol)], axis=-1)
            uo_ref[bi] = u
        else:
            u = u_ref[bi]
        ub = u.astype(BF16)

        for h in range(heads):
            ch = slice(h * hd, (h + 1) * hd)
            gates = _dot(ub[:, ch], wg_ref[h])
            tr = jnp.tanh(gates[:, :hd] + bg_ref[0:1, ch])
            ti = jnp.tanh(gates[:, hd:] + bg_ref[1:2, ch])
            a = jnp.exp(half_clam[:, ch] + half_clam[:, ch] * tr)
            m = jnp.maximum((1.0 - a) * (1.0 + a), 0.0)
            bx = (m * lax.rsqrt(jnp.maximum(m, SQRT_GUARD))) * ((0.5 + 0.5 * ti) * u[:, ch])
            for jj in range(hd // LANES):
                j = h * (hd // LANES) + jj
                lanes = slice(jj * LANES, (jj + 1) * LANES)
                at_sc.at[bi][pl.ds(j, t, stride=SUBLANES), :] = a[:, lanes]
                bt_sc.at[bi][pl.ds(j, t, stride=SUBLANES), :] = bx[:, lanes]

    def step(s, hs):
        tpos = (t - 1 - s) if reverse else s
        base = pl.multiple_of(tpos * SUBLANES, SUBLANES)
        out = []
        for bi in range(nbat):
            hh = at_sc[bi, pl.ds(base, SUBLANES), :] * hs[bi] + bt_sc[bi, pl.ds(base, SUBLANES), :]
            ht_sc[bi, pl.ds(base, SUBLANES), :] = hh
            out.append(hh)
        return tuple(out)

    hfin = lax.fori_loop(0, t, step, tuple(carry[bi] for bi in range(nbat)), unroll=SCAN_UNROLL)
    for bi in range(nbat):
        carry[bi] = hfin[bi]
        hfin_ref[bi] = hfin[bi]
        for j in range(ncol):
            lanes = slice(j * LANES, (j + 1) * LANES)
            rec = _from_time_major(ht_sc.at[bi], t, j)
            if gated:
                o_ref[bi, :, lanes] = (g_ref[bi, :, lanes].astype(F32)
                                       * (other_ref[bi, :, lanes] + rec)).astype(o_ref.dtype)
            else:
                o_ref[bi, :, lanes] = rec


def _rglru(src, layer, direction, params, h0, other=None, gate=None):
    conv_w, conv_b, w_gate, b_gate, lam = params
    reverse = bool(direction)
    conv = not reverse
    gated = other is not None
    heads = w_gate.shape[2]
    b = src.shape[0]
    if conv:
        l, w = src.shape[1] // SUBLANES, SUBLANES * LANES
    else:
        l, w = src.shape[1:]
    nbat = 2 if b % 2 == 0 else 1
    t = _tile(l, SCAN_CHUNK)
    n = l // t

    def pos(i):
        return (n - 1 - i) if reverse else i

    main = pl.BlockSpec((nbat, t, w), lambda bi, i: (bi, pos(i), 0))
    state = pl.BlockSpec((nbat, SUBLANES, LANES), lambda bi, i: (bi, 0, 0))
    if conv:
        in_specs = [
            pl.BlockSpec((nbat, t * SUBLANES, LANES), lambda bi, i: (bi, pos(i), 0)),
            pl.BlockSpec((nbat, 2 * SUBLANES, LANES),
                         lambda bi, i: (bi, jnp.maximum(pos(i) * (t // 2) - 1, 0), 0)),
            pl.BlockSpec((nbat, SUBLANES, LANES),
                         lambda bi, i: (bi, jnp.minimum((pos(i) + 1) * t, l - 1), 0)),
            _layer_block(conv_w, layer), _layer_block(conv_b, layer)]
        args = [src, src, src, conv_w, conv_b]
    else:
        in_specs, args = [main], [src]
    in_specs += [_layer_block(w_gate, layer, direction), _layer_block(b_gate, layer, direction),
                 _layer_block(lam, layer, direction), state]
    args += [w_gate, b_gate, lam, h0]
    if gated:
        in_specs += [main, main]
        args += [other, gate]
    out_specs = [main] + ([main] if conv else []) + [state]
    out_shape = ([jax.ShapeDtypeStruct((b, l, w), BF16 if gated else F32)]
                 + ([jax.ShapeDtypeStruct((b, l, w), F32)] if conv else [])
                 + [jax.ShapeDtypeStruct((b, SUBLANES, LANES), F32)])
    tmaj = pltpu.VMEM((nbat, t * SUBLANES, LANES), F32)
    scratch = ([tmaj] if conv else []) + [tmaj, tmaj, tmaj, pltpu.VMEM((nbat, SUBLANES, LANES), F32)]
    return pl.pallas_call(
        functools.partial(_rglru_kernel, reverse=reverse, conv=conv, gated=gated, heads=heads),
        grid=(b // nbat, n),
        in_specs=in_specs,
        out_specs=out_specs,
        out_shape=out_shape,
        scratch_shapes=scratch,
        compiler_params=_params("arbitrary", "arbitrary"),
        name="rglru_bwd" if reverse else "rglru_fwd",
    )(*args)


def _outproj_kernel(four_ref, rec_ref, w_ref, x_ref, mod_ref, gain_ref, xo_ref, h_ref):
    tm = x_ref.shape[0]
    sub = min(tm, PROJ_SUB_ROWS)
    post = gain_ref[1] * _mod(mod_ref, 2)
    pre = gain_ref[2] * (1.0 + _mod(mod_ref, 4))
    shift = _mod(mod_ref, 3)
    for s in range(tm // sub):
        rows = slice(s * sub, (s + 1) * sub)
        lhs = jnp.concatenate([four_ref[rows, :], rec_ref[rows, :]], axis=-1)
        y = _dot(lhs, w_ref[...])
        xn = x_ref[rows, :] + _rms(y, post)
        xo_ref[rows, :] = xn
        h_ref[rows, :] = (_rms(xn, pre) + shift).astype(h_ref.dtype)


def _outproj(four, rec, w_out, x, mods, row_of, gains, layer):
    b, l, d = x.shape
    fw, lw = four.shape[-1], rec.shape[-1]
    tm = _tile(l, ROW_TILE)
    row = lambda bi, i: (bi, i, 0)
    return pl.pallas_call(
        _outproj_kernel,
        grid=(b, l // tm),
        in_specs=[pl.BlockSpec((None, tm, fw), row), pl.BlockSpec((None, tm, lw), row),
                  _layer_block(w_out, layer), pl.BlockSpec((None, tm, d), row),
                  _mod_spec(mods, layer, row_of), _layer_block(gains, layer)],
        out_specs=[pl.BlockSpec((None, tm, d), row), pl.BlockSpec((None, tm, d), row)],
        out_shape=[jax.ShapeDtypeStruct((b, l, d), F32), jax.ShapeDtypeStruct((b, l, d), BF16)],
        compiler_params=_params("arbitrary", "arbitrary"),
        name="out_proj",
    )(four, rec, w_out, x, mods, gains)


def _ffn_kernel(h_ref, wg_ref, wu_ref, wo_ref, x_ref, mod_ref, gain_ref, o_ref, *, nj):
    j = pl.program_id(2)
    tm, d = o_ref.shape
    sub = min(tm, FFN_SUB_ROWS)
    post = gain_ref[3] * _mod(mod_ref, 5)

    def chunk(first, last):
        for s in range(tm // sub):
            rows = slice(s * sub, (s + 1) * sub)
            h = h_ref[rows, :]
            g = _dot(h, wg_ref[...])
            u = _dot(h, wu_ref[...])
            act = (g * jax.nn.sigmoid(g) * u).astype(BF16)
            for c in range(0, d, FFN_OUT_COLS):
                cols = slice(c, c + FFN_OUT_COLS)
                part = _dot(act, wo_ref[:, cols])
                o_ref[rows, cols] = part if first else o_ref[rows, cols] + part
            if last:
                o_ref[rows, :] = x_ref[rows, :] + _rms(o_ref[rows, :], post)

    if nj == 1:
        chunk(True, True)
    else:
        pl.when(j == 0)(functools.partial(chunk, True, False))
        pl.when((j > 0) & (j < nj - 1))(functools.partial(chunk, False, False))
        pl.when(j == nj - 1)(functools.partial(chunk, False, True))


def _ffn(h, w_ffn_in, w_ffn_out, x, mods, row_of, gains, layer):
    b, l, d = x.shape
    dff = w_ffn_out.shape[1]
    tm = _tile(l, FFN_ROW_TILE)
    tf = _tile(dff, FFN_COL_TILE)
    nj = dff // tf
    row = lambda bi, i, j: (bi, i, 0)
    return pl.pallas_call(
        functools.partial(_ffn_kernel, nj=nj),
        grid=(b, l // tm, nj),
        in_specs=[pl.BlockSpec((None, tm, d), row),
                  pl.BlockSpec((None, d, tf), lambda bi, i, j: (layer, 0, j)),
                  pl.BlockSpec((None, d, tf), lambda bi, i, j: (layer, 0, nj + j)),
                  pl.BlockSpec((None, tf, d), lambda bi, i, j: (layer, j, 0)),
                  pl.BlockSpec((None, tm, d), row),
                  _mod_spec(mods, layer, row_of), _layer_block(gains, layer)],
        out_specs=pl.BlockSpec((None, tm, d), row),
        out_shape=jax.ShapeDtypeStruct((b, l, d), F32),
        compiler_params=_params("arbitrary", "arbitrary", "arbitrary", vmem=60 * 1024 * 1024),
        name="ffn",
    )(h, w_ffn_in, w_ffn_in, w_ffn_out, x, mods, gains)


def kernel(x, c, ctx, c_ctx, w_ada, b_ada, norm_g, w_in, w_four, conv_w, conv_b, lru_w, lru_b, lru_lam,
           w_out, w_ffn_in, w_ffn_out):
    depth = w_ada.shape[0]
    bsz, _, d = x.shape
    lc = ctx.shape[1]
    fw = w_four.shape[1] * w_four.shape[2]
    lw = conv_w.shape[-1]

    pad = (-(bsz + 1)) % SUBLANES
    cs = jnp.concatenate([c, c_ctx[None, :], jnp.zeros((pad, d), F32)], axis=0)
    mods = _ada(cs, w_ada, b_ada)
    mods = mods.reshape(depth, mods.shape[1], 1, N_MOD * d)
    lat_row = lambda bi: bi
    ctx_row = lambda bi: bsz

    gains = norm_g.reshape(depth, norm_g.shape[1], 1, d)
    w_in_b = w_in.astype(BF16)
    w_out_b = w_out.astype(BF16)
    w_ffn_in_b = w_ffn_in.astype(BF16)
    w_ffn_out_b = w_ffn_out.astype(BF16)
    w_four_b = w_four.astype(BF16)
    w_gate_b = (0.5 * jnp.concatenate([lru_w[:, :, 0], lru_w[:, :, 1]], axis=-1)).astype(BF16)
    scan_params = (conv_w.reshape(depth, conv_w.shape[1], SUBLANES, LANES),
                   conv_b.reshape(depth, SUBLANES, LANES), w_gate_b, 0.5 * lru_b,
                   lru_lam.reshape(depth, 2, 1, lw))
    zero_state = jnp.zeros((bsz, SUBLANES, LANES), F32)

    ctx = ctx.reshape(1, bsz * lc, d)
    per_seq = lambda a: a.reshape(bsz, -1, a.shape[-1])
    flat = lambda a: a.reshape(1, bsz * lc, a.shape[-1])

    for layer in range(depth):
        last = layer == depth - 1
        f_lat, rt_lat, g_lat = _inproj(x, mods, lat_row, gains, w_in_b, layer, fw, lw)
        f_ctx, rt_ctx, g_ctx = _inproj(ctx, mods, ctx_row, gains, w_in_b, layer, fw, lw)

        recf_ctx, u_ctx, hf = _rglru(per_seq(rt_ctx), layer, 0, scan_params, zero_state)
        if last:
            _, hb = _rglru(u_ctx, layer, 1, scan_params, zero_state)
        else:
            rec_ctx, hb = _rglru(u_ctx, layer, 1, scan_params, zero_state, recf_ctx, per_seq(g_ctx))
        recf_lat, u_lat, _ = _rglru(rt_lat, layer, 0, scan_params, hf)
        rec_lat, _ = _rglru(u_lat, layer, 1, scan_params, hb, recf_lat, g_lat)

        four_lat = _fourier(f_lat, w_four_b, layer, GRID_W)
        x, h_lat = _outproj(four_lat, rec_lat, w_out_b, x, mods, lat_row, gains, layer)
        x = _ffn(h_lat, w_ffn_in_b, w_ffn_out_b, x, mods, lat_row, gains, layer)
        if not last:
            four_ctx = _fourier(per_seq(f_ctx), w_four_b, layer, None)
            ctx, h_ctx = _outproj(flat(four_ctx), flat(rec_ctx), w_out_b, ctx, mods, ctx_row, gains, layer)
            ctx = _ffn(h_ctx, w_ffn_in_b, w_ffn_out_b, ctx, mods, ctx_row, gains, layer)
    return x
```

```python
import functools
import math

import numpy as np
import jax
import jax.numpy as jnp
from jax import lax
from jax.experimental import pallas as pl
from jax.experimental.pallas import tpu as pltpu

GRID_W = 64
CONV_LEFT = 2
LRU_C = 8.0
EPS = 1e-6
N_MOD = 6

F32 = jnp.float32
BF16 = jnp.bfloat16

V7X_VMEM_LIMIT_BYTES = 56 * 1024 * 1024
SUBLANES = 8
LANES = 128
ROW_PITCH_PAD = 8
FOURIER_COL_UNROLL = 32
FOURIER_Q_BATCH = 16
IN_ROW_TILE = 1024
ROW_TILE = 512
PROJ_SUB_ROWS = 256
FFN_ROW_TILE = 1024
FFN_SUB_ROWS = 512
FFN_COL_TILE = 512
FFN_OUT_COLS = 512
SCAN_CHUNK = 512
SCAN_UNROLL = 8
SQRT_GUARD = 1e-30


def _params(*sem, vmem=V7X_VMEM_LIMIT_BYTES):
    return pltpu.CompilerParams(dimension_semantics=sem, vmem_limit_bytes=vmem)


def _dot(a, b):
    return jnp.dot(a, b, preferred_element_type=F32)


def _tile(n, pref):
    t = min(n, pref)
    while n % t:
        t //= 2
    return t


def _rms(x, gain):
    ms = jnp.mean(x * x, axis=-1, keepdims=True)
    return x * lax.rsqrt(ms + EPS) * gain


def _resident(shape):
    nd = len(shape)
    return pl.BlockSpec(shape, lambda *_: (0,) * nd, pipeline_mode=pl.Buffered(1))


def _layer_block(arr, *lead):
    nlead = len(lead)
    rest = arr.shape[nlead:]
    idx = tuple(lead) + (0,) * len(rest)
    return pl.BlockSpec((None,) * nlead + tuple(rest), lambda *_: idx, pipeline_mode=pl.Buffered(1))


def _mod_spec(mods, layer, row_of):
    return pl.BlockSpec((None, None, 1, mods.shape[-1]), lambda bi, *_: (layer, row_of(bi), 0, 0))


def _mod(mod_ref, k):
    d = mod_ref.shape[-1] // N_MOD
    return mod_ref[:, k * d:(k + 1) * d]


def _ada_kernel(c_ref, w_ref, b_ref, o_ref):
    c = c_ref[...]
    s = (c * jax.nn.sigmoid(c)).astype(BF16)
    o_ref[...] = _dot(s, w_ref[...].astype(BF16)) + b_ref[...]


def _ada(cs, w_ada, b_ada):
    depth, d, nm = w_ada.shape
    rows = cs.shape[0]
    tn = _tile(nm, 1024)
    return pl.pallas_call(
        _ada_kernel,
        grid=(depth, nm // tn),
        in_specs=[
            pl.BlockSpec((rows, d), lambda l, j: (0, 0)),
            pl.BlockSpec((None, d, tn), lambda l, j: (l, 0, j)),
            pl.BlockSpec((None, 1, tn), lambda l, j: (l, 0, j)),
        ],
        out_specs=pl.BlockSpec((None, rows, tn), lambda l, j: (l, 0, j)),
        out_shape=jax.ShapeDtypeStruct((depth, rows, nm), F32),
        compiler_params=_params("arbitrary", "arbitrary"),
        name="ada_mod",
    )(cs, w_ada, b_ada.reshape(depth, 1, nm))


def _inproj_kernel(x_ref, mod_ref, gain_ref, w_ref, f_ref, r_ref, g_ref):
    tm = x_ref.shape[0]
    nf = f_ref.shape[-1]
    nr = g_ref.shape[-1]
    sub = min(tm, PROJ_SUB_ROWS)
    gain = gain_ref[0] * (1.0 + _mod(mod_ref, 1))
    shift = _mod(mod_ref, 0)
    for s in range(tm // sub):
        rows = slice(s * sub, (s + 1) * sub)
        h = (_rms(x_ref[rows, :], gain) + shift).astype(BF16)
        g_ref[rows, :] = jax.nn.gelu(_dot(h, w_ref[:, nf + nr:])).astype(g_ref.dtype)
        r_ref[rows, :] = _dot(h, w_ref[:, nf:nf + nr])
        f_ref[rows, :] = _dot(h, w_ref[:, :nf]).astype(f_ref.dtype)


def _inproj(x, mods, row_of, gains, w_in, layer, fw, lw):
    b, l, d = x.shape
    tm = _tile(l, IN_ROW_TILE)
    row = lambda bi, i: (bi, i, 0)
    return pl.pallas_call(
        _inproj_kernel,
        grid=(b, l // tm),
        in_specs=[pl.BlockSpec((None, tm, d), row), _mod_spec(mods, layer, row_of),
                  _layer_block(gains, layer), _layer_block(w_in, layer)],
        out_specs=[pl.BlockSpec((None, tm, fw), row), pl.BlockSpec((None, tm, lw), row),
                   pl.BlockSpec((None, tm, lw), row)],
        out_shape=[jax.ShapeDtypeStruct((b, l, fw), BF16), jax.ShapeDtypeStruct((b, l, lw), F32),
                   jax.ShapeDtypeStruct((b, l, lw), BF16)],
        compiler_params=_params("arbitrary", "arbitrary"),
        name="in_proj",
    )(x, mods, gains, w_in)


def _dft(n):
    k = np.arange(n)
    ang = 2.0 * np.pi * ((k[:, None] * k[None, :]) % n) / n
    return np.cos(ang) / math.sqrt(n), np.sin(ang) / math.sqrt(n)


def _fold_channel_dft(ck_ref, sk_ref, wf_ref):
    wf = wf_ref[...]
    return _dot(ck_ref[...], wf).astype(BF16), (-_dot(sk_ref[...], wf)).astype(BF16)


def _fourier_grid_kernel(u_ref, mc_ref, mr_ref, ck_ref, sk_ref, wf_ref, o_ref, pc_ref, ps_ref, *,
                         rows, cols, pitch):
    gd = u_ref.shape[-1]
    nslab = gd // LANES
    wa, wb = _fold_channel_dft(ck_ref, sk_ref, wf_ref)
    wab = jnp.concatenate([wa, wb], axis=0)
    mc = mc_ref[...]
    mr = mr_ref[...]

    def col_stage(r, carry):
        src = pl.multiple_of(r * cols, cols)
        dst = pl.multiple_of(r * pitch, SUBLANES)
        t = _dot(mc, u_ref[pl.ds(src, cols), :])
        for s in range(nslab):
            lanes = slice(s * LANES, (s + 1) * LANES)
            pc_ref[s, pl.ds(dst, cols), :] = t[:cols, lanes]
            ps_ref[s, pl.ds(dst, cols), :] = t[cols:, lanes]
        return carry

    lax.fori_loop(0, rows, col_stage, 0, unroll=FOURIER_COL_UNROLL)

    def gather(ref, q):
        return jnp.concatenate([ref.at[s][pl.ds(q, rows, stride=pitch), :] for s in range(nslab)],
                               axis=-1).astype(BF16)

    def row_stage(it, carry):
        q0 = it * FOURIER_Q_BATCH
        x = jnp.concatenate(
            [jnp.concatenate([gather(pc_ref, q0 + k), gather(ps_ref, q0 + k)], axis=0)
             for k in range(FOURIER_Q_BATCH)], axis=-1)
        y = _dot(mr, x).astype(BF16)
        lhs = jnp.concatenate(
            [jnp.concatenate([y[:rows, k * gd:(k + 1) * gd], y[rows:, k * gd:(k + 1) * gd]], axis=-1)
             for k in range(FOURIER_Q_BATCH)], axis=0)
        z = _dot(lhs, wab)
        for k in range(FOURIER_Q_BATCH):
            for s in range(nslab):
                pc_ref.at[s][pl.ds(q0 + k, rows, stride=pitch), :] = (
                    z[k * rows:(k + 1) * rows, s * LANES:(s + 1) * LANES])
        return carry

    lax.fori_loop(0, cols // FOURIER_Q_BATCH, row_stage, 0)

    def emit(r, carry):
        src = pl.multiple_of(r * pitch, SUBLANES)
        dst = pl.multiple_of(r * cols, cols)
        for s in range(nslab):
            o_ref[pl.ds(dst, cols), s * LANES:(s + 1) * LANES] = (
                pc_ref[s, pl.ds(src, cols), :].astype(o_ref.dtype))
        return carry

    lax.fori_loop(0, rows, emit, 0, unroll=FOURIER_COL_UNROLL)


def _fourier_seq_kernel(u_ref, ml_ref, ck_ref, sk_ref, wf_ref, o_ref):
    l = u_ref.shape[0]
    wa, wb = _fold_channel_dft(ck_ref, sk_ref, wf_ref)
    y = _dot(ml_ref[...], u_ref[...])
    z = _dot(y[:l].astype(BF16), wa) + _dot(y[l:].astype(BF16), wb)
    o_ref[...] = z.astype(o_ref.dtype)


def _fourier(u, w_four, layer, grid_w):
    b, l, fw = u.shape
    _, g, gd, _ = w_four.shape
    ck, sk = _dft(gd)
    ck, sk = jnp.asarray(ck, BF16), jnp.asarray(sk, BF16)
    blk = pl.BlockSpec((None, l, gd), lambda bi, gi: (bi, 0, gi))
    wspec = pl.BlockSpec((None, None, gd, gd), lambda bi, gi: (layer, gi, 0, 0))
    common = dict(
        grid=(b, g),
        out_specs=blk,
        out_shape=jax.ShapeDtypeStruct((b, l, fw), BF16),
        compiler_params=_params("arbitrary", "arbitrary"),
    )
    if grid_w is None:
        cl, sl = _dft(l)
        ml = jnp.asarray(np.concatenate([cl, sl], axis=0), BF16)
        return pl.pallas_call(
            _fourier_seq_kernel,
            in_specs=[blk, _resident(ml.shape), _resident(ck.shape), _resident(sk.shape), wspec],
            name="fourier_seq", **common,
        )(u, ml, ck, sk, w_four)
    cols = grid_w
    rows = l // cols
    pitch = cols + ROW_PITCH_PAD
    cc, sc = _dft(cols)
    cr, sr = _dft(rows)
    mc = jnp.asarray(np.concatenate([cc, sc], axis=0), BF16)
    mr = jnp.asarray(np.block([[cr, -sr], [sr, cr]]), BF16)
    scratch = pltpu.VMEM((gd // LANES, rows * pitch, LANES), F32)
    return pl.pallas_call(
        functools.partial(_fourier_grid_kernel, rows=rows, cols=cols, pitch=pitch),
        in_specs=[blk, _resident(mc.shape), _resident(mr.shape), _resident(ck.shape),
                  _resident(sk.shape), wspec],
        scratch_shapes=[scratch, scratch],
        name="fourier_grid", **common,
    )(u, mc, mr, ck, sk, w_four)


def _to_time_major(dst, x):
    t = x.shape[0]
    for j in range(x.shape[1] // LANES):
        dst[pl.ds(j, t, stride=SUBLANES), :] = x[:, j * LANES:(j + 1) * LANES]


def _from_time_major(src, t, j):
    return src[pl.ds(j, t, stride=SUBLANES), :]


def _rglru_kernel(*refs, reverse, conv, gated, heads):
    refs = list(refs)
    if conv:
        r_ref, rp_ref, rn_ref, cw_ref, cb_ref = refs[:5]
        del refs[:5]
    else:
        u_ref = refs.pop(0)
    wg_ref, bg_ref, lam_ref, h0_ref = refs[:4]
    del refs[:4]
    if gated:
        other_ref, g_ref = refs[:2]
        del refs[:2]
    o_ref = refs.pop(0)
    if conv:
        uo_ref = refs.pop(0)
    hfin_ref = refs.pop(0)
    if conv:
        rt_sc, ut_sc = refs[:2]
        del refs[:2]
    at_sc, bt_sc, ht_sc, carry = refs

    nbat, t, w = o_ref.shape
    hd = w // heads
    ncol = w // LANES
    i = pl.program_id(1)
    n = pl.num_programs(1)
    ci = (n - 1 - i) if reverse else i

    @pl.when(i == 0)
    def _():
        carry[...] = h0_ref[...]

    half_clam = (-0.5 * LRU_C * math.log2(math.e)) * jax.nn.softplus(-lam_ref[...])

    for bi in range(nbat):
        if conv:
            rt = rt_sc.at[bi]
            prev = jnp.where(ci == 0, 0.0, rp_ref[bi])
            nxt = jnp.where(ci == n - 1, 0.0, rn_ref[bi])
            for j in range(ncol):
                lanes = slice(j * LANES, (j + 1) * LANES)
                rt[pl.ds(j, SUBLANES, stride=SUBLANES), :] = prev[:, lanes]
                rt[pl.ds(SUBLANES * SUBLANES + j, t, stride=SUBLANES), :] = r_ref[bi, :, lanes]
                rt[pl.ds((SUBLANES + t) * SUBLANES + j, SUBLANES, stride=SUBLANES), :] = nxt[:, lanes]
            ut = cb_ref[...][None]
            for k in range(cw_ref.shape[0]):
                off = (SUBLANES - CONV_LEFT + k) * SUBLANES
                tap = rt_sc[bi, off:off + t * SUBLANES, :]
                ut = ut + cw_ref[k][None] * tap.reshape(t, SUBLANES, LANES)
            ut_sc[bi] = ut.reshape(t * SUBLANES, LANES)
            u = jnp.concatenate([_from_time_major(ut_sc.at[bi], t, j) for j in range(ncol)], axis=-1)
            uo_ref[bi] = u
        else:
            u = u_ref[bi]
        ub = u.astype(BF16)

        for h in range(heads):
            ch = slice(h * hd, (h + 1) * hd)
            gates = _dot(ub[:, ch], wg_ref[h])
            tr = jnp.tanh(gates[:, :hd] + bg_ref[0:1, ch])
            ti = jnp.tanh(gates[:, hd:] + bg_ref[1:2, ch])
            a = jnp.exp2(half_clam[:, ch] + half_clam[:, ch] * tr)
            m = jnp.maximum((1.0 - a) * (1.0 + a), 0.0)
            bx = (m * lax.rsqrt(jnp.maximum(m, SQRT_GUARD))) * ((0.5 + 0.5 * ti) * u[:, ch])
            for jj in range(hd // LANES):
                j = h * (hd // LANES) + jj
                lanes = slice(jj * LANES, (jj + 1) * LANES)
                at_sc.at[bi][pl.ds(j, t, stride=SUBLANES), :] = a[:, lanes]
                bt_sc.at[bi][pl.ds(j, t, stride=SUBLANES), :] = bx[:, lanes]

    def step(s, hs):
        tpos = (t - 1 - s) if reverse else s
        base = pl.multiple_of(tpos * SUBLANES, SUBLANES)
        out = []
        for bi in range(nbat):
            hh = at_sc[bi, pl.ds(base, SUBLANES), :] * hs[bi] + bt_sc[bi, pl.ds(base, SUBLANES), :]
            ht_sc[bi, pl.ds(base, SUBLANES), :] = hh
            out.append(hh)
        return tuple(out)

    hfin = lax.fori_loop(0, t, step, tuple(carry[bi] for bi in range(nbat)), unroll=SCAN_UNROLL)
    for bi in range(nbat):
        carry[bi] = hfin[bi]
        hfin_ref[bi] = hfin[bi]
        for j in range(ncol):
            lanes = slice(j * LANES, (j + 1) * LANES)
            rec = _from_time_major(ht_sc.at[bi], t, j)
            if gated:
                o_ref[bi, :, lanes] = (g_ref[bi, :, lanes].astype(F32)
                                       * (other_ref[bi, :, lanes] + rec)).astype(o_ref.dtype)
            else:
                o_ref[bi, :, lanes] = rec


def _rglru(src, layer, direction, params, h0, other=None, gate=None):
    conv_w, conv_b, w_gate, b_gate, lam = params
    reverse = bool(direction)
    conv = not reverse
    gated = other is not None
    heads = w_gate.shape[2]
    b, l, w = src.shape
    assert w == SUBLANES * LANES, "time-major layout packs the recurrent width into one (8, 128) tile"
    nbat = 2 if b % 2 == 0 else 1
    t = _tile(l, SCAN_CHUNK)
    n = l // t
    tb = t // SUBLANES
    nb = l // SUBLANES

    def pos(i):
        return (n - 1 - i) if reverse else i

    main = pl.BlockSpec((nbat, t, w), lambda bi, i: (bi, pos(i), 0))
    state = pl.BlockSpec((nbat, SUBLANES, LANES), lambda bi, i: (bi, 0, 0))
    in_specs, args = [main], [src]
    if conv:
        in_specs += [
            pl.BlockSpec((nbat, SUBLANES, w), lambda bi, i: (bi, jnp.maximum(pos(i) * tb - 1, 0), 0)),
            pl.BlockSpec((nbat, SUBLANES, w), lambda bi, i: (bi, jnp.minimum((pos(i) + 1) * tb, nb - 1), 0)),
            _layer_block(conv_w, layer), _layer_block(conv_b, layer)]
        args += [src, src, conv_w, conv_b]
    in_specs += [_layer_block(w_gate, layer, direction), _layer_block(b_gate, layer, direction),
                 _layer_block(lam, layer, direction), state]
    args += [w_gate, b_gate, lam, h0]
    if gated:
        in_specs += [main, main]
        args += [other, gate]
    out_specs = [main] + ([main] if conv else []) + [state]
    out_shape = ([jax.ShapeDtypeStruct((b, l, w), BF16 if gated else F32)]
                 + ([jax.ShapeDtypeStruct((b, l, w), F32)] if conv else [])
                 + [jax.ShapeDtypeStruct((b, SUBLANES, LANES), F32)])
    tmaj = pltpu.VMEM((nbat, t * SUBLANES, LANES), F32)
    halo = pltpu.VMEM((nbat, (t + 2 * SUBLANES) * SUBLANES, LANES), F32)
    scratch = ([halo, tmaj] if conv else []) + [tmaj, tmaj, tmaj, pltpu.VMEM((nbat, SUBLANES, LANES), F32)]
    return pl.pallas_call(
        functools.partial(_rglru_kernel, reverse=reverse, conv=conv, gated=gated, heads=heads),
        grid=(b // nbat, n),
        in_specs=in_specs,
        out_specs=out_specs,
        out_shape=out_shape,
        scratch_shapes=scratch,
        compiler_params=_params("arbitrary", "arbitrary"),
        name="rglru_bwd" if reverse else "rglru_fwd",
    )(*args)


def _outproj_kernel(four_ref, rec_ref, w_ref, x_ref, mod_ref, gain_ref, xo_ref, h_ref):
    tm = x_ref.shape[0]
    sub = min(tm, PROJ_SUB_ROWS)
    post = gain_ref[1] * _mod(mod_ref, 2)
    pre = gain_ref[2] * (1.0 + _mod(mod_ref, 4))
    shift = _mod(mod_ref, 3)
    for s in range(tm // sub):
        rows = slice(s * sub, (s + 1) * sub)
        lhs = jnp.concatenate([four_ref[rows, :], rec_ref[rows, :]], axis=-1)
        y = _dot(lhs, w_ref[...])
        xn = x_ref[rows, :] + _rms(y, post)
        xo_ref[rows, :] = xn
        h_ref[rows, :] = (_rms(xn, pre) + shift).astype(h_ref.dtype)


def _outproj(four, rec, w_out, x, mods, row_of, gains, layer):
    b, l, d = x.shape
    fw, lw = four.shape[-1], rec.shape[-1]
    tm = _tile(l, ROW_TILE)
    row = lambda bi, i: (bi, i, 0)
    return pl.pallas_call(
        _outproj_kernel,
        grid=(b, l // tm),
        in_specs=[pl.BlockSpec((None, tm, fw), row), pl.BlockSpec((None, tm, lw), row),
                  _layer_block(w_out, layer), pl.BlockSpec((None, tm, d), row),
                  _mod_spec(mods, layer, row_of), _layer_block(gains, layer)],
        out_specs=[pl.BlockSpec((None, tm, d), row), pl.BlockSpec((None, tm, d), row)],
        out_shape=[jax.ShapeDtypeStruct((b, l, d), F32), jax.ShapeDtypeStruct((b, l, d), BF16)],
        compiler_params=_params("arbitrary", "arbitrary"),
        name="out_proj",
    )(four, rec, w_out, x, mods, gains)


def _ffn_kernel(h_ref, wg_ref, wu_ref, wo_ref, x_ref, mod_ref, gain_ref, o_ref, *, nj):
    j = pl.program_id(2)
    tm, d = o_ref.shape
    sub = min(tm, FFN_SUB_ROWS)
    post = gain_ref[3] * _mod(mod_ref, 5)

    def chunk(first, last):
        for s in range(tm // sub):
            rows = slice(s * sub, (s + 1) * sub)
            h = h_ref[rows, :]
            g = _dot(h, wg_ref[...])
            u = _dot(h, wu_ref[...])
            act = (g * jax.nn.sigmoid(g) * u).astype(BF16)
            for c in range(0, d, FFN_OUT_COLS):
                cols = slice(c, c + FFN_OUT_COLS)
                part = _dot(act, wo_ref[:, cols])
                o_ref[rows, cols] = part if first else o_ref[rows, cols] + part
            if last:
                o_ref[rows, :] = x_ref[rows, :] + _rms(o_ref[rows, :], post)

    if nj == 1:
        chunk(True, True)
    else:
        pl.when(j == 0)(functools.partial(chunk, True, False))
        pl.when((j > 0) & (j < nj - 1))(functools.partial(chunk, False, False))
        pl.when(j == nj - 1)(functools.partial(chunk, False, True))


def _ffn(h, w_ffn_in, w_ffn_out, x, mods, row_of, gains, layer):
    b, l, d = x.shape
    dff = w_ffn_out.shape[1]
    tm = _tile(l, FFN_ROW_TILE)
    tf = _tile(dff, FFN_COL_TILE)
    nj = dff // tf
    row = lambda bi, i, j: (bi, i, 0)
    return pl.pallas_call(
        functools.partial(_ffn_kernel, nj=nj),
        grid=(b, l // tm, nj),
        in_specs=[pl.BlockSpec((None, tm, d), row),
                  pl.BlockSpec((None, d, tf), lambda bi, i, j: (layer, 0, j)),
                  pl.BlockSpec((None, d, tf), lambda bi, i, j: (layer, 0, nj + j)),
                  pl.BlockSpec((None, tf, d), lambda bi, i, j: (layer, j, 0)),
                  pl.BlockSpec((None, tm, d), row),
                  _mod_spec(mods, layer, row_of), _layer_block(gains, layer)],
        out_specs=pl.BlockSpec((None, tm, d), row),
        out_shape=jax.ShapeDtypeStruct((b, l, d), F32),
        compiler_params=_params("arbitrary", "arbitrary", "arbitrary", vmem=60 * 1024 * 1024),
        name="ffn",
    )(h, w_ffn_in, w_ffn_in, w_ffn_out, x, mods, gains)


def kernel(x, c, ctx, c_ctx, w_ada, b_ada, norm_g, w_in, w_four, conv_w, conv_b, lru_w, lru_b, lru_lam,
           w_out, w_ffn_in, w_ffn_out):
    depth = w_ada.shape[0]
    bsz, _, d = x.shape
    lc = ctx.shape[1]
    fw = w_four.shape[1] * w_four.shape[2]
    lw = conv_w.shape[-1]

    pad = (-(bsz + 1)) % SUBLANES
    cs = jnp.concatenate([c, c_ctx[None, :], jnp.zeros((pad, d), F32)], axis=0)
    mods = _ada(cs, w_ada, b_ada)
    mods = mods.reshape(depth, mods.shape[1], 1, N_MOD * d)
    lat_row = lambda bi: bi
    ctx_row = lambda bi: bsz

    gains = norm_g.reshape(depth, norm_g.shape[1], 1, d)
    w_in_b = w_in.astype(BF16)
    w_out_b = w_out.astype(BF16)
    w_ffn_in_b = w_ffn_in.astype(BF16)
    w_ffn_out_b = w_ffn_out.astype(BF16)
    w_four_b = w_four.astype(BF16)
    w_gate_b = (0.5 * jnp.concatenate([lru_w[:, :, 0], lru_w[:, :, 1]], axis=-1)).astype(BF16)
    scan_params = (conv_w.reshape(depth, conv_w.shape[1], SUBLANES, LANES),
                   conv_b.reshape(depth, SUBLANES, LANES), w_gate_b, 0.5 * lru_b,
                   lru_lam.reshape(depth, 2, 1, lw))
    zero_state = jnp.zeros((bsz, SUBLANES, LANES), F32)

    ctx = ctx.reshape(1, bsz * lc, d)
    per_seq = lambda a: a.reshape(bsz, -1, a.shape[-1])
    flat = lambda a: a.reshape(1, bsz * lc, a.shape[-1])

    for layer in range(depth):
        last = layer == depth - 1
        f_lat, r_lat, g_lat = _inproj(x, mods, lat_row, gains, w_in_b, layer, fw, lw)
        f_ctx, r_ctx, g_ctx = _inproj(ctx, mods, ctx_row, gains, w_in_b, layer, fw, lw)

        recf_ctx, u_ctx, hf = _rglru(per_seq(r_ctx), layer, 0, scan_params, zero_state)
        if last:
            _, hb = _rglru(u_ctx, layer, 1, scan_params, zero_state)
        else:
            rec_ctx, hb = _rglru(u_ctx, layer, 1, scan_params, zero_state, recf_ctx, per_seq(g_ctx))
        recf_lat, u_lat, _ = _rglru(r_lat, layer, 0, scan_params, hf)
        rec_lat, _ = _rglru(u_lat, layer, 1, scan_params, hb, recf_lat, g_lat)

        four_lat = _fourier(f_lat, w_four_b, layer, GRID_W)
        x, h_lat = _outproj(four_lat, rec_lat, w_out_b, x, mods, lat_row, gains, layer)
        x = _ffn(h_lat, w_ffn_in_b, w_ffn_out_b, x, mods, lat_row, gains, layer)
        if not last:
            four_ctx = _fourier(per_seq(f_ctx), w_four_b, layer, None)
            ctx, h_ctx = _outproj(flat(four_ctx), flat(rec_ctx), w_out_b, ctx, mods, ctx_row, gains, layer)
            ctx = _ffn(h_ctx, w_ffn_in_b, w_ffn_out_b, ctx, mods, ctx_row, gains, layer)
    return x
```

```python
import functools
import math

import numpy as np
import jax
import jax.numpy as jnp
from jax import lax
from jax.experimental import pallas as pl
from jax.experimental.pallas import tpu as pltpu

GRID_W = 64
CONV_LEFT = 2
LRU_C = 8.0
EPS = 1e-6
N_MOD = 6

F32 = jnp.float32
BF16 = jnp.bfloat16

V7X_VMEM_LIMIT_BYTES = 56 * 1024 * 1024
SUBLANES = 8
LANES = 128
ROW_PITCH_PAD = 8
FOURIER_COL_UNROLL = 32
FOURIER_Q_BATCH = 16
IN_ROW_TILE = 1024
ROW_TILE = 512
PROJ_SUB_ROWS = 256
OUT_SUB_ROWS = 128
FFN_ROW_TILE = 1024
FFN_SUB_ROWS = 512
FFN_COL_TILE = 512
FFN_OUT_COLS = 512
SCAN_CHUNK = 256
SCAN_SEQS = 4
SCAN_UNROLL = 8
SQRT_GUARD = 1e-30


def _params(*sem, vmem=V7X_VMEM_LIMIT_BYTES):
    return pltpu.CompilerParams(dimension_semantics=sem, vmem_limit_bytes=vmem)


def _dot(a, b):
    return jnp.dot(a, b, preferred_element_type=F32)


def _tile(n, pref):
    t = min(n, pref)
    while n % t:
        t //= 2
    return t


def _rms(x, gain):
    ms = jnp.mean(x * x, axis=-1, keepdims=True)
    return x * lax.rsqrt(ms + EPS) * gain


def _resident(shape):
    nd = len(shape)
    return pl.BlockSpec(shape, lambda *_: (0,) * nd, pipeline_mode=pl.Buffered(1))


def _layer_block(arr, *lead):
    nlead = len(lead)
    rest = arr.shape[nlead:]
    idx = tuple(lead) + (0,) * len(rest)
    return pl.BlockSpec((None,) * nlead + tuple(rest), lambda *_: idx, pipeline_mode=pl.Buffered(1))


def _mod_spec(mods, layer, row_of):
    return pl.BlockSpec((None, None, 1, mods.shape[-1]), lambda bi, *_: (layer, row_of(bi), 0, 0))


def _mod(mod_ref, k):
    d = mod_ref.shape[-1] // N_MOD
    return mod_ref[:, k * d:(k + 1) * d]


def _ada_kernel(c_ref, w_ref, b_ref, o_ref):
    c = c_ref[...]
    s = (c * jax.nn.sigmoid(c)).astype(BF16)
    o_ref[...] = _dot(s, w_ref[...].astype(BF16)) + b_ref[...]


def _ada(cs, w_ada, b_ada):
    depth, d, nm = w_ada.shape
    rows = cs.shape[0]
    tn = _tile(nm, 1024)
    return pl.pallas_call(
        _ada_kernel,
        grid=(depth, nm // tn),
        in_specs=[
            pl.BlockSpec((rows, d), lambda l, j: (0, 0)),
            pl.BlockSpec((None, d, tn), lambda l, j: (l, 0, j)),
            pl.BlockSpec((None, 1, tn), lambda l, j: (l, 0, j)),
        ],
        out_specs=pl.BlockSpec((None, rows, tn), lambda l, j: (l, 0, j)),
        out_shape=jax.ShapeDtypeStruct((depth, rows, nm), F32),
        compiler_params=_params("arbitrary", "arbitrary"),
        name="ada_mod",
    )(cs, w_ada, b_ada.reshape(depth, 1, nm))


def _inproj_kernel(x_ref, mod_ref, gain_ref, w_ref, f_ref, r_ref, g_ref):
    tm = x_ref.shape[0]
    nf = f_ref.shape[-1]
    nr = g_ref.shape[-1]
    sub = min(tm, PROJ_SUB_ROWS)
    gain = gain_ref[0] * (1.0 + _mod(mod_ref, 1))
    shift = _mod(mod_ref, 0)
    for s in range(tm // sub):
        rows = slice(s * sub, (s + 1) * sub)
        h = (_rms(x_ref[rows, :], gain) + shift).astype(BF16)
        g_ref[rows, :] = jax.nn.gelu(_dot(h, w_ref[:, nf + nr:])).astype(g_ref.dtype)
        r_ref[rows, :] = _dot(h, w_ref[:, nf:nf + nr])
        f_ref[rows, :] = _dot(h, w_ref[:, :nf]).astype(f_ref.dtype)


def _inproj(x, mods, row_of, gains, w_in, layer, fw, lw):
    b, l, d = x.shape
    tm = _tile(l, IN_ROW_TILE)
    row = lambda bi, i: (bi, i, 0)
    return pl.pallas_call(
        _inproj_kernel,
        grid=(b, l // tm),
        in_specs=[pl.BlockSpec((None, tm, d), row), _mod_spec(mods, layer, row_of),
                  _layer_block(gains, layer), _layer_block(w_in, layer)],
        out_specs=[pl.BlockSpec((None, tm, fw), row), pl.BlockSpec((None, tm, lw), row),
                   pl.BlockSpec((None, tm, lw), row)],
        out_shape=[jax.ShapeDtypeStruct((b, l, fw), BF16), jax.ShapeDtypeStruct((b, l, lw), F32),
                   jax.ShapeDtypeStruct((b, l, lw), BF16)],
        compiler_params=_params("arbitrary", "arbitrary"),
        name="in_proj",
    )(x, mods, gains, w_in)


def _dft(n):
    k = np.arange(n)
    ang = 2.0 * np.pi * ((k[:, None] * k[None, :]) % n) / n
    return np.cos(ang) / math.sqrt(n), np.sin(ang) / math.sqrt(n)


def _fold_channel_dft(ck_ref, sk_ref, wf_ref):
    wf = wf_ref[...]
    return _dot(ck_ref[...], wf).astype(BF16), (-_dot(sk_ref[...], wf)).astype(BF16)


def _fourier_grid_kernel(u_ref, mc_ref, mr_ref, ck_ref, sk_ref, wf_ref, o_ref, pc_ref, ps_ref, *,
                         rows, cols, pitch):
    gd = u_ref.shape[-1]
    nslab = gd // LANES
    wa, wb = _fold_channel_dft(ck_ref, sk_ref, wf_ref)
    wab = jnp.concatenate([wa, wb], axis=0)
    mc = mc_ref[...]
    mr = mr_ref[...]

    def col_stage(r, carry):
        src = pl.multiple_of(r * cols, cols)
        dst = pl.multiple_of(r * pitch, SUBLANES)
        t = _dot(mc, u_ref[pl.ds(src, cols), :])
        for s in range(nslab):
            lanes = slice(s * LANES, (s + 1) * LANES)
            pc_ref[s, pl.ds(dst, cols), :] = t[:cols, lanes]
            ps_ref[s, pl.ds(dst, cols), :] = t[cols:, lanes]
        return carry

    lax.fori_loop(0, rows, col_stage, 0, unroll=FOURIER_COL_UNROLL)

    def gather(ref, q):
        return jnp.concatenate([ref.at[s][pl.ds(q, rows, stride=pitch), :] for s in range(nslab)],
                               axis=-1).astype(BF16)

    def row_stage(it, carry):
        q0 = it * FOURIER_Q_BATCH
        x = jnp.concatenate(
            [jnp.concatenate([gather(pc_ref, q0 + k), gather(ps_ref, q0 + k)], axis=0)
             for k in range(FOURIER_Q_BATCH)], axis=-1)
        y = _dot(mr, x).astype(BF16)
        lhs = jnp.concatenate(
            [jnp.concatenate([y[:rows, k * gd:(k + 1) * gd], y[rows:, k * gd:(k + 1) * gd]], axis=-1)
             for k in range(FOURIER_Q_BATCH)], axis=0)
        z = _dot(lhs, wab)
        for k in range(FOURIER_Q_BATCH):
            for s in range(nslab):
                pc_ref.at[s][pl.ds(q0 + k, rows, stride=pitch), :] = (
                    z[k * rows:(k + 1) * rows, s * LANES:(s + 1) * LANES])
        return carry

    lax.fori_loop(0, cols // FOURIER_Q_BATCH, row_stage, 0)

    def emit(r, carry):
        src = pl.multiple_of(r * pitch, SUBLANES)
        dst = pl.multiple_of(r * cols, cols)
        for s in range(nslab):
            o_ref[pl.ds(dst, cols), s * LANES:(s + 1) * LANES] = (
                pc_ref[s, pl.ds(src, cols), :].astype(o_ref.dtype))
        return carry

    lax.fori_loop(0, rows, emit, 0, unroll=FOURIER_COL_UNROLL)


def _fourier_seq_kernel(u_ref, ml_ref, ck_ref, sk_ref, wf_ref, o_ref):
    l = u_ref.shape[0]
    wa, wb = _fold_channel_dft(ck_ref, sk_ref, wf_ref)
    y = _dot(ml_ref[...], u_ref[...])
    z = _dot(y[:l].astype(BF16), wa) + _dot(y[l:].astype(BF16), wb)
    o_ref[...] = z.astype(o_ref.dtype)


def _fourier(u, w_four, layer, grid_w):
    b, l, fw = u.shape
    _, g, gd, _ = w_four.shape
    ck, sk = _dft(gd)
    ck, sk = jnp.asarray(ck, BF16), jnp.asarray(sk, BF16)
    blk = pl.BlockSpec((None, l, gd), lambda bi, gi: (bi, 0, gi))
    wspec = pl.BlockSpec((None, None, gd, gd), lambda bi, gi: (layer, gi, 0, 0))
    common = dict(
        grid=(b, g),
        out_specs=blk,
        out_shape=jax.ShapeDtypeStruct((b, l, fw), BF16),
        compiler_params=_params("arbitrary", "arbitrary"),
    )
    if grid_w is None:
        cl, sl = _dft(l)
        ml = jnp.asarray(np.concatenate([cl, sl], axis=0), BF16)
        return pl.pallas_call(
            _fourier_seq_kernel,
            in_specs=[blk, _resident(ml.shape), _resident(ck.shape), _resident(sk.shape), wspec],
            name="fourier_seq", **common,
        )(u, ml, ck, sk, w_four)
    cols = grid_w
    rows = l // cols
    pitch = cols + ROW_PITCH_PAD
    cc, sc = _dft(cols)
    cr, sr = _dft(rows)
    mc = jnp.asarray(np.concatenate([cc, sc], axis=0), BF16)
    mr = jnp.asarray(np.block([[cr, -sr], [sr, cr]]), BF16)
    scratch = pltpu.VMEM((gd // LANES, rows * pitch, LANES), F32)
    return pl.pallas_call(
        functools.partial(_fourier_grid_kernel, rows=rows, cols=cols, pitch=pitch),
        in_specs=[blk, _resident(mc.shape), _resident(mr.shape), _resident(ck.shape),
                  _resident(sk.shape), wspec],
        scratch_shapes=[scratch, scratch],
        name="fourier_grid", **common,
    )(u, mc, mr, ck, sk, w_four)


def _to_time_major(dst, x):
    t = x.shape[0]
    for j in range(x.shape[1] // LANES):
        dst[pl.ds(j, t, stride=SUBLANES), :] = x[:, j * LANES:(j + 1) * LANES]


def _from_time_major(src, t, j):
    return src[pl.ds(j, t, stride=SUBLANES), :]


def _rglru_kernel(*refs, reverse, conv, gated, heads):
    refs = list(refs)
    if conv:
        r_ref, rp_ref, rn_ref, cw_ref, cb_ref = refs[:5]
        del refs[:5]
    else:
        u_ref = refs.pop(0)
    wg_ref, bg_ref, lam_ref, h0_ref = refs[:4]
    del refs[:4]
    if gated:
        other_ref, g_ref = refs[:2]
        del refs[:2]
    o_ref = refs.pop(0)
    if conv:
        uo_ref = refs.pop(0)
    hfin_ref = refs.pop(0)
    if conv:
        rt_sc, ut_sc = refs[:2]
        del refs[:2]
    at_sc, bt_sc, ht_sc, carry = refs

    nbat, t, w = o_ref.shape
    hd = w // heads
    ncol = w // LANES
    i = pl.program_id(1)
    n = pl.num_programs(1)
    ci = (n - 1 - i) if reverse else i

    @pl.when(i == 0)
    def _():
        carry[...] = h0_ref[...]

    half_clam = (-0.5 * LRU_C * math.log2(math.e)) * jax.nn.softplus(-lam_ref[...])

    for bi in range(nbat):
        if conv:
            rt = rt_sc.at[bi]
            prev = jnp.where(ci == 0, 0.0, rp_ref[bi])
            nxt = jnp.where(ci == n - 1, 0.0, rn_ref[bi])
            for j in range(ncol):
                lanes = slice(j * LANES, (j + 1) * LANES)
                rt[pl.ds(j, SUBLANES, stride=SUBLANES), :] = prev[:, lanes]
                rt[pl.ds(SUBLANES * SUBLANES + j, t, stride=SUBLANES), :] = r_ref[bi, :, lanes]
                rt[pl.ds((SUBLANES + t) * SUBLANES + j, SUBLANES, stride=SUBLANES), :] = nxt[:, lanes]
            ut = cb_ref[...][None]
            for k in range(cw_ref.shape[0]):
                off = (SUBLANES - CONV_LEFT + k) * SUBLANES
                tap = rt_sc[bi, off:off + t * SUBLANES, :]
                ut = ut + cw_ref[k][None] * tap.reshape(t, SUBLANES, LANES)
            ut_sc[bi] = ut.reshape(t * SUBLANES, LANES)
            u = jnp.concatenate([_from_time_major(ut_sc.at[bi], t, j) for j in range(ncol)], axis=-1)
            uo_ref[bi] = u
        else:
            u = u_ref[bi]
        ub = u.astype(BF16)

        for h in range(heads):
            ch = slice(h * hd, (h + 1) * hd)
            gates = _dot(ub[:, ch], wg_ref[h])
            tr = jnp.tanh(gates[:, :hd] + bg_ref[0:1, ch])
            ti = jnp.tanh(gates[:, hd:] + bg_ref[1:2, ch])
            a = jnp.exp2(half_clam[:, ch] + half_clam[:, ch] * tr)
            m = jnp.maximum((1.0 - a) * (1.0 + a), 0.0)
            bx = (m * lax.rsqrt(jnp.maximum(m, SQRT_GUARD))) * ((0.5 + 0.5 * ti) * u[:, ch])
            for jj in range(hd // LANES):
                j = h * (hd // LANES) + jj
                lanes = slice(jj * LANES, (jj + 1) * LANES)
                at_sc.at[bi][pl.ds(j, t, stride=SUBLANES), :] = a[:, lanes]
                bt_sc.at[bi][pl.ds(j, t, stride=SUBLANES), :] = bx[:, lanes]

    grp = min(t, SCAN_UNROLL)

    def steps(gi, hs):
        gpos = (t // grp - 1 - gi) if reverse else gi
        base = pl.multiple_of(gpos * grp * SUBLANES, grp * SUBLANES)
        hs = list(hs)
        for k in range(grp):
            rows = pl.ds(base + ((grp - 1 - k) if reverse else k) * SUBLANES, SUBLANES)
            for bi in range(nbat):
                hs[bi] = at_sc[bi, rows, :] * hs[bi] + bt_sc[bi, rows, :]
                ht_sc[bi, rows, :] = hs[bi]
        return tuple(hs)

    hfin = lax.fori_loop(0, t // grp, steps, tuple(carry[bi] for bi in range(nbat)))
    for bi in range(nbat):
        carry[bi] = hfin[bi]
        hfin_ref[bi] = hfin[bi]
        for j in range(ncol):
            lanes = slice(j * LANES, (j + 1) * LANES)
            rec = _from_time_major(ht_sc.at[bi], t, j)
            if gated:
                o_ref[bi, :, lanes] = (g_ref[bi, :, lanes].astype(F32)
                                       * (other_ref[bi, :, lanes] + rec)).astype(o_ref.dtype)
            else:
                o_ref[bi, :, lanes] = rec


def _rglru(src, layer, direction, params, h0, other=None, gate=None):
    conv_w, conv_b, w_gate, b_gate, lam = params
    reverse = bool(direction)
    conv = not reverse
    gated = other is not None
    heads = w_gate.shape[2]
    b, l, w = src.shape
    assert w == SUBLANES * LANES, "time-major layout packs the recurrent width into one (8, 128) tile"
    nbat = max(k for k in (1, 2, SCAN_SEQS) if b % k == 0)
    t = _tile(l, SCAN_CHUNK)
    n = l // t
    tb = t // SUBLANES
    nb = l // SUBLANES

    def pos(i):
        return (n - 1 - i) if reverse else i

    main = pl.BlockSpec((nbat, t, w), lambda bi, i: (bi, pos(i), 0))
    state = pl.BlockSpec((nbat, SUBLANES, LANES), lambda bi, i: (bi, 0, 0))
    in_specs, args = [main], [src]
    if conv:
        in_specs += [
            pl.BlockSpec((nbat, SUBLANES, w), lambda bi, i: (bi, jnp.maximum(pos(i) * tb - 1, 0), 0)),
            pl.BlockSpec((nbat, SUBLANES, w), lambda bi, i: (bi, jnp.minimum((pos(i) + 1) * tb, nb - 1), 0)),
            _layer_block(conv_w, layer), _layer_block(conv_b, layer)]
        args += [src, src, conv_w, conv_b]
    in_specs += [_layer_block(w_gate, layer, direction), _layer_block(b_gate, layer, direction),
                 _layer_block(lam, layer, direction), state]
    args += [w_gate, b_gate, lam, h0]
    if gated:
        in_specs += [main, main]
        args += [other, gate]
    out_specs = [main] + ([main] if conv else []) + [state]
    out_shape = ([jax.ShapeDtypeStruct((b, l, w), BF16 if gated else F32)]
                 + ([jax.ShapeDtypeStruct((b, l, w), F32)] if conv else [])
                 + [jax.ShapeDtypeStruct((b, SUBLANES, LANES), F32)])
    tmaj = pltpu.VMEM((nbat, t * SUBLANES, LANES), F32)
    halo = pltpu.VMEM((nbat, (t + 2 * SUBLANES) * SUBLANES, LANES), F32)
    scratch = ([halo, tmaj] if conv else []) + [tmaj, tmaj, tmaj, pltpu.VMEM((nbat, SUBLANES, LANES), F32)]
    return pl.pallas_call(
        functools.partial(_rglru_kernel, reverse=reverse, conv=conv, gated=gated, heads=heads),
        grid=(b // nbat, n),
        in_specs=in_specs,
        out_specs=out_specs,
        out_shape=out_shape,
        scratch_shapes=scratch,
        compiler_params=_params("arbitrary", "arbitrary"),
        name="rglru_bwd" if reverse else "rglru_fwd",
    )(*args)


def _outproj_kernel(four_ref, rec_ref, w_ref, x_ref, mod_ref, gain_ref, xo_ref, h_ref):
    tm = x_ref.shape[0]
    sub = min(tm, OUT_SUB_ROWS)
    post = gain_ref[1] * _mod(mod_ref, 2)
    pre = gain_ref[2] * (1.0 + _mod(mod_ref, 4))
    shift = _mod(mod_ref, 3)
    for s in range(tm // sub):
        rows = slice(s * sub, (s + 1) * sub)
        lhs = jnp.concatenate([four_ref[rows, :], rec_ref[rows, :]], axis=-1)
        y = _dot(lhs, w_ref[...])
        xn = x_ref[rows, :] + _rms(y, post)
        xo_ref[rows, :] = xn
        h_ref[rows, :] = (_rms(xn, pre) + shift).astype(h_ref.dtype)


def _outproj(four, rec, w_out, x, mods, row_of, gains, layer):
    b, l, d = x.shape
    fw, lw = four.shape[-1], rec.shape[-1]
    tm = _tile(l, ROW_TILE)
    row = lambda bi, i: (bi, i, 0)
    return pl.pallas_call(
        _outproj_kernel,
        grid=(b, l // tm),
        in_specs=[pl.BlockSpec((None, tm, fw), row), pl.BlockSpec((None, tm, lw), row),
                  _layer_block(w_out, layer), pl.BlockSpec((None, tm, d), row),
                  _mod_spec(mods, layer, row_of), _layer_block(gains, layer)],
        out_specs=[pl.BlockSpec((None, tm, d), row), pl.BlockSpec((None, tm, d), row)],
        out_shape=[jax.ShapeDtypeStruct((b, l, d), F32), jax.ShapeDtypeStruct((b, l, d), BF16)],
        compiler_params=_params("arbitrary", "arbitrary"),
        name="out_proj",
    )(four, rec, w_out, x, mods, gains)


def _ffn_kernel(h_ref, wg_ref, wu_ref, wo_ref, x_ref, mod_ref, gain_ref, o_ref, *, nj):
    j = pl.program_id(2)
    tm, d = o_ref.shape
    sub = min(tm, FFN_SUB_ROWS)
    post = gain_ref[3] * _mod(mod_ref, 5)

    def chunk(first, last):
        for s in range(tm // sub):
            rows = slice(s * sub, (s + 1) * sub)
            h = h_ref[rows, :]
            g = _dot(h, wg_ref[...])
            u = _dot(h, wu_ref[...])
            act = (g * jax.nn.sigmoid(g) * u).astype(BF16)
            for c in range(0, d, FFN_OUT_COLS):
                cols = slice(c, c + FFN_OUT_COLS)
                part = _dot(act, wo_ref[:, cols])
                o_ref[rows, cols] = part if first else o_ref[rows, cols] + part
            if last:
                o_ref[rows, :] = x_ref[rows, :] + _rms(o_ref[rows, :], post)

    if nj == 1:
        chunk(True, True)
    else:
        pl.when(j == 0)(functools.partial(chunk, True, False))
        pl.when((j > 0) & (j < nj - 1))(functools.partial(chunk, False, False))
        pl.when(j == nj - 1)(functools.partial(chunk, False, True))


def _ffn(h, w_ffn_in, w_ffn_out, x, mods, row_of, gains, layer):
    b, l, d = x.shape
    dff = w_ffn_out.shape[1]
    tm = _tile(l, FFN_ROW_TILE)
    tf = _tile(dff, FFN_COL_TILE)
    nj = dff // tf
    row = lambda bi, i, j: (bi, i, 0)
    return pl.pallas_call(
        functools.partial(_ffn_kernel, nj=nj),
        grid=(b, l // tm, nj),
        in_specs=[pl.BlockSpec((None, tm, d), row),
                  pl.BlockSpec((None, d, tf), lambda bi, i, j: (layer, 0, j)),
                  pl.BlockSpec((None, d, tf), lambda bi, i, j: (layer, 0, nj + j)),
                  pl.BlockSpec((None, tf, d), lambda bi, i, j: (layer, j, 0)),
                  pl.BlockSpec((None, tm, d), row),
                  _mod_spec(mods, layer, row_of), _layer_block(gains, layer)],
        out_specs=pl.BlockSpec((None, tm, d), row),
        out_shape=jax.ShapeDtypeStruct((b, l, d), F32),
        compiler_params=_params("arbitrary", "arbitrary", "arbitrary", vmem=60 * 1024 * 1024),
        name="ffn",
    )(h, w_ffn_in, w_ffn_in, w_ffn_out, x, mods, gains)


def kernel(x, c, ctx, c_ctx, w_ada, b_ada, norm_g, w_in, w_four, conv_w, conv_b, lru_w, lru_b, lru_lam,
           w_out, w_ffn_in, w_ffn_out):
    depth = w_ada.shape[0]
    bsz, _, d = x.shape
    lc = ctx.shape[1]
    fw = w_four.shape[1] * w_four.shape[2]
    lw = conv_w.shape[-1]

    pad = (-(bsz + 1)) % SUBLANES
    cs = jnp.concatenate([c, c_ctx[None, :], jnp.zeros((pad, d), F32)], axis=0)
    mods = _ada(cs, w_ada, b_ada)
    mods = mods.reshape(depth, mods.shape[1], 1, N_MOD * d)
    lat_row = lambda bi: bi
    ctx_row = lambda bi: bsz

    gains = norm_g.reshape(depth, norm_g.shape[1], 1, d)
    w_in_b = w_in.astype(BF16)
    w_out_b = w_out.astype(BF16)
    w_ffn_in_b = w_ffn_in.astype(BF16)
    w_ffn_out_b = w_ffn_out.astype(BF16)
    w_four_b = w_four.astype(BF16)
    w_gate_b = (0.5 * jnp.concatenate([lru_w[:, :, 0], lru_w[:, :, 1]], axis=-1)).astype(BF16)
    scan_params = (conv_w.reshape(depth, conv_w.shape[1], SUBLANES, LANES),
                   conv_b.reshape(depth, SUBLANES, LANES), w_gate_b, 0.5 * lru_b,
                   lru_lam.reshape(depth, 2, 1, lw))
    zero_state = jnp.zeros((bsz, SUBLANES, LANES), F32)

    ctx = ctx.reshape(1, bsz * lc, d)
    per_seq = lambda a: a.reshape(bsz, -1, a.shape[-1])
    flat = lambda a: a.reshape(1, bsz * lc, a.shape[-1])

    for layer in range(depth):
        last = layer == depth - 1
        f_lat, r_lat, g_lat = _inproj(x, mods, lat_row, gains, w_in_b, layer, fw, lw)
        f_ctx, r_ctx, g_ctx = _inproj(ctx, mods, ctx_row, gains, w_in_b, layer, fw, lw)

        recf_ctx, u_ctx, hf = _rglru(per_seq(r_ctx), layer, 0, scan_params, zero_state)
        if last:
            _, hb = _rglru(u_ctx, layer, 1, scan_params, zero_state)
        else:
            rec_ctx, hb = _rglru(u_ctx, layer, 1, scan_params, zero_state, recf_ctx, per_seq(g_ctx))
        recf_lat, u_lat, _ = _rglru(r_lat, layer, 0, scan_params, hf)
        rec_lat, _ = _rglru(u_lat, layer, 1, scan_params, hb, recf_lat, g_lat)

        four_lat = _fourier(f_lat, w_four_b, layer, GRID_W)
        x, h_lat = _outproj(four_lat, rec_lat, w_out_b, x, mods, lat_row, gains, layer)
        x = _ffn(h_lat, w_ffn_in_b, w_ffn_out_b, x, mods, lat_row, gains, layer)
        if not last:
            four_ctx = _fourier(per_seq(f_ctx), w_four_b, layer, None)
            ctx, h_ctx = _outproj(flat(four_ctx), flat(rec_ctx), w_out_b, ctx, mods, ctx_row, gains, layer)
            ctx = _ffn(h_ctx, w_ffn_in_b, w_ffn_out_b, ctx, mods, ctx_row, gains, layer)
    return x
```

```python
import functools
import math

import numpy as np
import jax
import jax.numpy as jnp
from jax import lax
from jax.experimental import pallas as pl
from jax.experimental.pallas import tpu as pltpu

GRID_W = 64
CONV_LEFT = 2
LRU_C = 8.0
EPS = 1e-6
N_MOD = 6

F32 = jnp.float32
BF16 = jnp.bfloat16

V7X_VMEM_LIMIT_BYTES = 56 * 1024 * 1024
SUBLANES = 8
LANES = 128
ROW_PITCH_PAD = 8
FOURIER_COL_UNROLL = 32
FOURIER_Q_BATCH = 16
IN_ROW_TILE = 1024
ROW_TILE = 512
PROJ_SUB_ROWS = 256
OUT_SUB_ROWS = 128
FFN_ROW_TILE = 1024
FFN_SUB_ROWS = 512
FFN_COL_TILE = 512
FFN_OUT_COLS = 512
SCAN_CHUNK = 256
SCAN_SEQS = 4
SCAN_UNROLL = 8
SQRT_GUARD = 1e-30


def _params(*sem, vmem=V7X_VMEM_LIMIT_BYTES):
    return pltpu.CompilerParams(dimension_semantics=sem, vmem_limit_bytes=vmem)


def _dot(a, b):
    return jnp.dot(a, b, preferred_element_type=F32)


def _tile(n, pref):
    t = min(n, pref)
    while n % t:
        t //= 2
    return t


def _rms(x, gain):
    ms = jnp.mean(x * x, axis=-1, keepdims=True)
    return x * lax.rsqrt(ms + EPS) * gain


def _resident(shape):
    nd = len(shape)
    return pl.BlockSpec(shape, lambda *_: (0,) * nd, pipeline_mode=pl.Buffered(1))


def _layer_block(arr, *lead):
    nlead = len(lead)
    rest = arr.shape[nlead:]
    idx = tuple(lead) + (0,) * len(rest)
    return pl.BlockSpec((None,) * nlead + tuple(rest), lambda *_: idx, pipeline_mode=pl.Buffered(1))


def _mod_spec(mods, layer, row_of):
    return pl.BlockSpec((None, None, 1, mods.shape[-1]), lambda bi, *_: (layer, row_of(bi), 0, 0))


def _mod(mod_ref, k):
    d = mod_ref.shape[-1] // N_MOD
    return mod_ref[:, k * d:(k + 1) * d]


def _ada_kernel(c_ref, w_ref, b_ref, o_ref):
    c = c_ref[...]
    s = (c * jax.nn.sigmoid(c)).astype(BF16)
    o_ref[...] = _dot(s, w_ref[...].astype(BF16)) + b_ref[...]


def _ada(cs, w_ada, b_ada):
    depth, d, nm = w_ada.shape
    rows = cs.shape[0]
    tn = _tile(nm, 1024)
    return pl.pallas_call(
        _ada_kernel,
        grid=(depth, nm // tn),
        in_specs=[
            pl.BlockSpec((rows, d), lambda l, j: (0, 0)),
            pl.BlockSpec((None, d, tn), lambda l, j: (l, 0, j)),
            pl.BlockSpec((None, 1, tn), lambda l, j: (l, 0, j)),
        ],
        out_specs=pl.BlockSpec((None, rows, tn), lambda l, j: (l, 0, j)),
        out_shape=jax.ShapeDtypeStruct((depth, rows, nm), F32),
        compiler_params=_params("arbitrary", "arbitrary"),
        name="ada_mod",
    )(cs, w_ada, b_ada.reshape(depth, 1, nm))


def _inproj_kernel(x_ref, mod_ref, gain_ref, w_ref, f_ref, r_ref, g_ref):
    tm = x_ref.shape[0]
    ngrp, _, gd = f_ref.shape
    nf = ngrp * gd
    nr = g_ref.shape[-1]
    sub = min(tm, PROJ_SUB_ROWS)
    gain = gain_ref[0] * (1.0 + _mod(mod_ref, 1))
    shift = _mod(mod_ref, 0)
    for s in range(tm // sub):
        rows = slice(s * sub, (s + 1) * sub)
        h = (_rms(x_ref[rows, :], gain) + shift).astype(BF16)
        g_ref[rows, :] = jax.nn.gelu(_dot(h, w_ref[:, nf + nr:])).astype(g_ref.dtype)
        r_ref[rows, :] = _dot(h, w_ref[:, nf:nf + nr])
        f = _dot(h, w_ref[:, :nf]).astype(f_ref.dtype)
        for gi in range(ngrp):
            f_ref[gi, rows, :] = f[:, gi * gd:(gi + 1) * gd]


def _inproj(x, mods, row_of, gains, w_in, layer, ngrp, gd, lw):
    b, l, d = x.shape
    tm = _tile(l, IN_ROW_TILE)
    row = lambda bi, i: (bi, i, 0)
    return pl.pallas_call(
        _inproj_kernel,
        grid=(b, l // tm),
        in_specs=[pl.BlockSpec((None, tm, d), row), _mod_spec(mods, layer, row_of),
                  _layer_block(gains, layer), _layer_block(w_in, layer)],
        out_specs=[pl.BlockSpec((None, ngrp, tm, gd), lambda bi, i: (bi, 0, i, 0)),
                   pl.BlockSpec((None, tm, lw), row), pl.BlockSpec((None, tm, lw), row)],
        out_shape=[jax.ShapeDtypeStruct((b, ngrp, l, gd), BF16), jax.ShapeDtypeStruct((b, l, lw), F32),
                   jax.ShapeDtypeStruct((b, l, lw), BF16)],
        compiler_params=_params("arbitrary", "arbitrary"),
        name="in_proj",
    )(x, mods, gains, w_in)


def _dft(n):
    k = np.arange(n)
    ang = 2.0 * np.pi * ((k[:, None] * k[None, :]) % n) / n
    return np.cos(ang) / math.sqrt(n), np.sin(ang) / math.sqrt(n)


def _fold_channel_dft(ck_ref, sk_ref, wf_ref):
    wf = wf_ref[...]
    return _dot(ck_ref[...], wf).astype(BF16), (-_dot(sk_ref[...], wf)).astype(BF16)


def _fourier_grid_kernel(u_ref, mc_ref, mr_ref, ck_ref, sk_ref, wf_ref, o_ref, pc_ref, ps_ref, *,
                         rows, cols, pitch):
    gd = u_ref.shape[-1]
    nslab = gd // LANES
    wa, wb = _fold_channel_dft(ck_ref, sk_ref, wf_ref)
    wab = jnp.concatenate([wa, wb], axis=0)
    mc = mc_ref[...]
    mr = mr_ref[...]

    def col_stage(r, carry):
        src = pl.multiple_of(r * cols, cols)
        dst = pl.multiple_of(r * pitch, SUBLANES)
        t = _dot(mc, u_ref[pl.ds(src, cols), :])
        for s in range(nslab):
            lanes = slice(s * LANES, (s + 1) * LANES)
            pc_ref[s, pl.ds(dst, cols), :] = t[:cols, lanes]
            ps_ref[s, pl.ds(dst, cols), :] = t[cols:, lanes]
        return carry

    lax.fori_loop(0, rows, col_stage, 0, unroll=FOURIER_COL_UNROLL)

    def gather(ref, q):
        return jnp.concatenate([ref.at[s][pl.ds(q, rows, stride=pitch), :] for s in range(nslab)],
                               axis=-1).astype(BF16)

    def row_stage(it, carry):
        q0 = it * FOURIER_Q_BATCH
        x = jnp.concatenate(
            [jnp.concatenate([gather(pc_ref, q0 + k), gather(ps_ref, q0 + k)], axis=0)
             for k in range(FOURIER_Q_BATCH)], axis=-1)
        y = _dot(mr, x).astype(BF16)
        lhs = jnp.concatenate(
            [jnp.concatenate([y[:rows, k * gd:(k + 1) * gd], y[rows:, k * gd:(k + 1) * gd]], axis=-1)
             for k in range(FOURIER_Q_BATCH)], axis=0)
        z = _dot(lhs, wab)
        for k in range(FOURIER_Q_BATCH):
            for s in range(nslab):
                pc_ref.at[s][pl.ds(q0 + k, rows, stride=pitch), :] = (
                    z[k * rows:(k + 1) * rows, s * LANES:(s + 1) * LANES])
        return carry

    lax.fori_loop(0, cols // FOURIER_Q_BATCH, row_stage, 0)

    def emit(r, carry):
        src = pl.multiple_of(r * pitch, SUBLANES)
        dst = pl.multiple_of(r * cols, cols)
        for s in range(nslab):
            o_ref[pl.ds(dst, cols), s * LANES:(s + 1) * LANES] = (
                pc_ref[s, pl.ds(src, cols), :].astype(o_ref.dtype))
        return carry

    lax.fori_loop(0, rows, emit, 0, unroll=FOURIER_COL_UNROLL)


def _fourier_seq_kernel(u_ref, ml_ref, ck_ref, sk_ref, wf_ref, o_ref):
    l = u_ref.shape[0]
    wa, wb = _fold_channel_dft(ck_ref, sk_ref, wf_ref)
    y = _dot(ml_ref[...], u_ref[...])
    z = _dot(y[:l].astype(BF16), wa) + _dot(y[l:].astype(BF16), wb)
    o_ref[...] = z.astype(o_ref.dtype)


def _fourier(u, w_four, layer, l, grid_w):
    a, g, rows_total, gd = u.shape
    per = rows_total // l
    ck, sk = _dft(gd)
    ck, sk = jnp.asarray(ck, BF16), jnp.asarray(sk, BF16)
    blk = pl.BlockSpec((None, None, l, gd), lambda si, gi: (si // per, gi, si % per, 0))
    wspec = pl.BlockSpec((None, None, gd, gd), lambda si, gi: (layer, gi, 0, 0))
    common = dict(
        grid=(a * per, g),
        out_specs=blk,
        out_shape=jax.ShapeDtypeStruct(u.shape, BF16),
        compiler_params=_params("arbitrary", "arbitrary"),
    )
    if grid_w is None:
        cl, sl = _dft(l)
        ml = jnp.asarray(np.concatenate([cl, sl], axis=0), BF16)
        return pl.pallas_call(
            _fourier_seq_kernel,
            in_specs=[blk, _resident(ml.shape), _resident(ck.shape), _resident(sk.shape), wspec],
            name="fourier_seq", **common,
        )(u, ml, ck, sk, w_four)
    cols = grid_w
    rows = l // cols
    pitch = cols + ROW_PITCH_PAD
    cc, sc = _dft(cols)
    cr, sr = _dft(rows)
    mc = jnp.asarray(np.concatenate([cc, sc], axis=0), BF16)
    mr = jnp.asarray(np.block([[cr, -sr], [sr, cr]]), BF16)
    scratch = pltpu.VMEM((gd // LANES, rows * pitch, LANES), F32)
    return pl.pallas_call(
        functools.partial(_fourier_grid_kernel, rows=rows, cols=cols, pitch=pitch),
        in_specs=[blk, _resident(mc.shape), _resident(mr.shape), _resident(ck.shape),
                  _resident(sk.shape), wspec],
        scratch_shapes=[scratch, scratch],
        name="fourier_grid", **common,
    )(u, mc, mr, ck, sk, w_four)


def _to_time_major(dst, x):
    t = x.shape[0]
    for j in range(x.shape[1] // LANES):
        dst[pl.ds(j, t, stride=SUBLANES), :] = x[:, j * LANES:(j + 1) * LANES]


def _from_time_major(src, t, j):
    return src[pl.ds(j, t, stride=SUBLANES), :]


def _rglru_kernel(*refs, reverse, conv, gated, heads):
    refs = list(refs)
    if conv:
        r_ref, rp_ref, rn_ref, cw_ref, cb_ref = refs[:5]
        del refs[:5]
    else:
        u_ref = refs.pop(0)
    wg_ref, bg_ref, lam_ref, h0_ref = refs[:4]
    del refs[:4]
    if gated:
        other_ref, g_ref = refs[:2]
        del refs[:2]
    o_ref = refs.pop(0)
    if conv:
        uo_ref = refs.pop(0)
    hfin_ref = refs.pop(0)
    if conv:
        rt_sc, ut_sc = refs[:2]
        del refs[:2]
    at_sc, bt_sc, ht_sc, carry = refs

    nbat, t, w = o_ref.shape
    hd = w // heads
    ncol = w // LANES
    i = pl.program_id(1)
    n = pl.num_programs(1)
    ci = (n - 1 - i) if reverse else i

    @pl.when(i == 0)
    def _():
        carry[...] = h0_ref[...]

    half_clam = (-0.5 * LRU_C * math.log2(math.e)) * jax.nn.softplus(-lam_ref[...])

    for bi in range(nbat):
        if conv:
            rt = rt_sc.at[bi]
            prev = jnp.where(ci == 0, 0.0, rp_ref[bi])
            nxt = jnp.where(ci == n - 1, 0.0, rn_ref[bi])
            for j in range(ncol):
                lanes = slice(j * LANES, (j + 1) * LANES)
                rt[pl.ds(j, SUBLANES, stride=SUBLANES), :] = prev[:, lanes]
                rt[pl.ds(SUBLANES * SUBLANES + j, t, stride=SUBLANES), :] = r_ref[bi, :, lanes]
                rt[pl.ds((SUBLANES + t) * SUBLANES + j, SUBLANES, stride=SUBLANES), :] = nxt[:, lanes]
            ut = cb_ref[...][None]
            for k in range(cw_ref.shape[0]):
                off = (SUBLANES - CONV_LEFT + k) * SUBLANES
                tap = rt_sc[bi, off:off + t * SUBLANES, :]
                ut = ut + cw_ref[k][None] * tap.reshape(t, SUBLANES, LANES)
            ut_sc[bi] = ut.reshape(t * SUBLANES, LANES)
            u = jnp.concatenate([_from_time_major(ut_sc.at[bi], t, j) for j in range(ncol)], axis=-1)
            uo_ref[bi] = u
        else:
            u = u_ref[bi]
        ub = u.astype(BF16)

        for h in range(heads):
            ch = slice(h * hd, (h + 1) * hd)
            gates = _dot(ub[:, ch], wg_ref[h])
            tr = jnp.tanh(gates[:, :hd] + bg_ref[0:1, ch])
            ti = jnp.tanh(gates[:, hd:] + bg_ref[1:2, ch])
            a = jnp.exp2(half_clam[:, ch] + half_clam[:, ch] * tr)
            m = jnp.maximum((1.0 - a) * (1.0 + a), 0.0)
            bx = (m * lax.rsqrt(jnp.maximum(m, SQRT_GUARD))) * ((0.5 + 0.5 * ti) * u[:, ch])
            for jj in range(hd // LANES):
                j = h * (hd // LANES) + jj
                lanes = slice(jj * LANES, (jj + 1) * LANES)
                at_sc.at[bi][pl.ds(j, t, stride=SUBLANES), :] = a[:, lanes]
                bt_sc.at[bi][pl.ds(j, t, stride=SUBLANES), :] = bx[:, lanes]

    grp = min(t, SCAN_UNROLL)

    def steps(gi, hs):
        gpos = (t // grp - 1 - gi) if reverse else gi
        base = pl.multiple_of(gpos * grp * SUBLANES, grp * SUBLANES)
        hs = list(hs)
        for k in range(grp):
            rows = pl.ds(base + ((grp - 1 - k) if reverse else k) * SUBLANES, SUBLANES)
            for bi in range(nbat):
                hs[bi] = at_sc[bi, rows, :] * hs[bi] + bt_sc[bi, rows, :]
                ht_sc[bi, rows, :] = hs[bi]
        return tuple(hs)

    hfin = lax.fori_loop(0, t // grp, steps, tuple(carry[bi] for bi in range(nbat)))
    for bi in range(nbat):
        carry[bi] = hfin[bi]
        hfin_ref[bi] = hfin[bi]
        for j in range(ncol):
            lanes = slice(j * LANES, (j + 1) * LANES)
            rec = _from_time_major(ht_sc.at[bi], t, j)
            if gated:
                o_ref[bi, :, lanes] = (g_ref[bi, :, lanes].astype(F32)
                                       * (other_ref[bi, :, lanes] + rec)).astype(o_ref.dtype)
            else:
                o_ref[bi, :, lanes] = rec


def _rglru(src, layer, direction, params, h0, other=None, gate=None):
    conv_w, conv_b, w_gate, b_gate, lam = params
    reverse = bool(direction)
    conv = not reverse
    gated = other is not None
    heads = w_gate.shape[2]
    b, l, w = src.shape
    assert w == SUBLANES * LANES, "time-major layout packs the recurrent width into one (8, 128) tile"
    nbat = max(k for k in (1, 2, SCAN_SEQS) if b % k == 0)
    t = _tile(l, SCAN_CHUNK)
    n = l // t
    tb = t // SUBLANES
    nb = l // SUBLANES

    def pos(i):
        return (n - 1 - i) if reverse else i

    main = pl.BlockSpec((nbat, t, w), lambda bi, i: (bi, pos(i), 0))
    state = pl.BlockSpec((nbat, SUBLANES, LANES), lambda bi, i: (bi, 0, 0))
    in_specs, args = [main], [src]
    if conv:
        in_specs += [
            pl.BlockSpec((nbat, SUBLANES, w), lambda bi, i: (bi, jnp.maximum(pos(i) * tb - 1, 0), 0)),
            pl.BlockSpec((nbat, SUBLANES, w), lambda bi, i: (bi, jnp.minimum((pos(i) + 1) * tb, nb - 1), 0)),
            _layer_block(conv_w, layer), _layer_block(conv_b, layer)]
        args += [src, src, conv_w, conv_b]
    in_specs += [_layer_block(w_gate, layer, direction), _layer_block(b_gate, layer, direction),
                 _layer_block(lam, layer, direction), state]
    args += [w_gate, b_gate, lam, h0]
    if gated:
        in_specs += [main, main]
        args += [other, gate]
    out_specs = [main] + ([main] if conv else []) + [state]
    out_shape = ([jax.ShapeDtypeStruct((b, l, w), BF16 if gated else F32)]
                 + ([jax.ShapeDtypeStruct((b, l, w), F32)] if conv else [])
                 + [jax.ShapeDtypeStruct((b, SUBLANES, LANES), F32)])
    tmaj = pltpu.VMEM((nbat, t * SUBLANES, LANES), F32)
    halo = pltpu.VMEM((nbat, (t + 2 * SUBLANES) * SUBLANES, LANES), F32)
    scratch = ([halo, tmaj] if conv else []) + [tmaj, tmaj, tmaj, pltpu.VMEM((nbat, SUBLANES, LANES), F32)]
    return pl.pallas_call(
        functools.partial(_rglru_kernel, reverse=reverse, conv=conv, gated=gated, heads=heads),
        grid=(b // nbat, n),
        in_specs=in_specs,
        out_specs=out_specs,
        out_shape=out_shape,
        scratch_shapes=scratch,
        compiler_params=_params("arbitrary", "arbitrary"),
        name="rglru_bwd" if reverse else "rglru_fwd",
    )(*args)


def _outproj_kernel(four_ref, rec_ref, w_ref, x_ref, mod_ref, gain_ref, xo_ref, h_ref):
    tm = x_ref.shape[0]
    sub = min(tm, OUT_SUB_ROWS)
    post = gain_ref[1] * _mod(mod_ref, 2)
    pre = gain_ref[2] * (1.0 + _mod(mod_ref, 4))
    shift = _mod(mod_ref, 3)
    for s in range(tm // sub):
        rows = slice(s * sub, (s + 1) * sub)
        lhs = jnp.concatenate([four_ref[gi, rows, :] for gi in range(four_ref.shape[0])]
                              + [rec_ref[rows, :]], axis=-1)
        y = _dot(lhs, w_ref[...])
        xn = x_ref[rows, :] + _rms(y, post)
        xo_ref[rows, :] = xn
        h_ref[rows, :] = (_rms(xn, pre) + shift).astype(h_ref.dtype)


def _outproj(four, rec, w_out, x, mods, row_of, gains, layer):
    b, l, d = x.shape
    _, ngrp, _, gd = four.shape
    lw = rec.shape[-1]
    tm = _tile(l, ROW_TILE)
    row = lambda bi, i: (bi, i, 0)
    return pl.pallas_call(
        _outproj_kernel,
        grid=(b, l // tm),
        in_specs=[pl.BlockSpec((None, ngrp, tm, gd), lambda bi, i: (bi, 0, i, 0)),
                  pl.BlockSpec((None, tm, lw), row),
                  _layer_block(w_out, layer), pl.BlockSpec((None, tm, d), row),
                  _mod_spec(mods, layer, row_of), _layer_block(gains, layer)],
        out_specs=[pl.BlockSpec((None, tm, d), row), pl.BlockSpec((None, tm, d), row)],
        out_shape=[jax.ShapeDtypeStruct((b, l, d), F32), jax.ShapeDtypeStruct((b, l, d), BF16)],
        compiler_params=_params("arbitrary", "arbitrary"),
        name="out_proj",
    )(four, rec, w_out, x, mods, gains)


def _ffn_kernel(h_ref, wg_ref, wu_ref, wo_ref, x_ref, mod_ref, gain_ref, o_ref, *, nj):
    j = pl.program_id(2)
    tm, d = o_ref.shape
    sub = min(tm, FFN_SUB_ROWS)
    post = gain_ref[3] * _mod(mod_ref, 5)

    def chunk(first, last):
        for s in range(tm // sub):
            rows = slice(s * sub, (s + 1) * sub)
            h = h_ref[rows, :]
            g = _dot(h, wg_ref[...])
            u = _dot(h, wu_ref[...])
            act = (g * jax.nn.sigmoid(g) * u).astype(BF16)
            for c in range(0, d, FFN_OUT_COLS):
                cols = slice(c, c + FFN_OUT_COLS)
                part = _dot(act, wo_ref[:, cols])
                o_ref[rows, cols] = part if first else o_ref[rows, cols] + part
            if last:
                o_ref[rows, :] = x_ref[rows, :] + _rms(o_ref[rows, :], post)

    if nj == 1:
        chunk(True, True)
    else:
        pl.when(j == 0)(functools.partial(chunk, True, False))
        pl.when((j > 0) & (j < nj - 1))(functools.partial(chunk, False, False))
        pl.when(j == nj - 1)(functools.partial(chunk, False, True))


def _ffn(h, w_ffn_in, w_ffn_out, x, mods, row_of, gains, layer):
    b, l, d = x.shape
    dff = w_ffn_out.shape[1]
    tm = _tile(l, FFN_ROW_TILE)
    tf = _tile(dff, FFN_COL_TILE)
    nj = dff // tf
    row = lambda bi, i, j: (bi, i, 0)
    return pl.pallas_call(
        functools.partial(_ffn_kernel, nj=nj),
        grid=(b, l // tm, nj),
        in_specs=[pl.BlockSpec((None, tm, d), row),
                  pl.BlockSpec((None, d, tf), lambda bi, i, j: (layer, 0, j)),
                  pl.BlockSpec((None, d, tf), lambda bi, i, j: (layer, 0, nj + j)),
                  pl.BlockSpec((None, tf, d), lambda bi, i, j: (layer, j, 0)),
                  pl.BlockSpec((None, tm, d), row),
                  _mod_spec(mods, layer, row_of), _layer_block(gains, layer)],
        out_specs=pl.BlockSpec((None, tm, d), row),
        out_shape=jax.ShapeDtypeStruct((b, l, d), F32),
        compiler_params=_params("arbitrary", "arbitrary", "arbitrary", vmem=60 * 1024 * 1024),
        name="ffn",
    )(h, w_ffn_in, w_ffn_in, w_ffn_out, x, mods, gains)


def kernel(x, c, ctx, c_ctx, w_ada, b_ada, norm_g, w_in, w_four, conv_w, conv_b, lru_w, lru_b, lru_lam,
           w_out, w_ffn_in, w_ffn_out):
    depth = w_ada.shape[0]
    bsz, _, d = x.shape
    lc = ctx.shape[1]
    ngrp, gd = w_four.shape[1:3]
    lw = conv_w.shape[-1]

    pad = (-(bsz + 1)) % SUBLANES
    cs = jnp.concatenate([c, c_ctx[None, :], jnp.zeros((pad, d), F32)], axis=0)
    mods = _ada(cs, w_ada, b_ada)
    mods = mods.reshape(depth, mods.shape[1], 1, N_MOD * d)
    lat_row = lambda bi: bi
    ctx_row = lambda bi: bsz

    gains = norm_g.reshape(depth, norm_g.shape[1], 1, d)
    w_in_b = w_in.astype(BF16)
    w_out_b = w_out.astype(BF16)
    w_ffn_in_b = w_ffn_in.astype(BF16)
    w_ffn_out_b = w_ffn_out.astype(BF16)
    w_four_b = w_four.astype(BF16)
    w_gate_b = (0.5 * jnp.concatenate([lru_w[:, :, 0], lru_w[:, :, 1]], axis=-1)).astype(BF16)
    scan_params = (conv_w.reshape(depth, conv_w.shape[1], SUBLANES, LANES),
                   conv_b.reshape(depth, SUBLANES, LANES), w_gate_b, 0.5 * lru_b,
                   lru_lam.reshape(depth, 2, 1, lw))
    zero_state = jnp.zeros((bsz, SUBLANES, LANES), F32)

    ctx = ctx.reshape(1, bsz * lc, d)
    per_seq = lambda a: a.reshape(bsz, -1, a.shape[-1])
    flat = lambda a: a.reshape(1, bsz * lc, a.shape[-1])

    for layer in range(depth):
        last = layer == depth - 1
        f_lat, r_lat, g_lat = _inproj(x, mods, lat_row, gains, w_in_b, layer, ngrp, gd, lw)
        f_ctx, r_ctx, g_ctx = _inproj(ctx, mods, ctx_row, gains, w_in_b, layer, ngrp, gd, lw)

        recf_ctx, u_ctx, hf = _rglru(per_seq(r_ctx), layer, 0, scan_params, zero_state)
        if last:
            _, hb = _rglru(u_ctx, layer, 1, scan_params, zero_state)
        else:
            rec_ctx, hb = _rglru(u_ctx, layer, 1, scan_params, zero_state, recf_ctx, per_seq(g_ctx))
        recf_lat, u_lat, _ = _rglru(r_lat, layer, 0, scan_params, hf)
        rec_lat, _ = _rglru(u_lat, layer, 1, scan_params, hb, recf_lat, g_lat)

        four_lat = _fourier(f_lat, w_four_b, layer, x.shape[1], GRID_W)
        x, h_lat = _outproj(four_lat, rec_lat, w_out_b, x, mods, lat_row, gains, layer)
        x = _ffn(h_lat, w_ffn_in_b, w_ffn_out_b, x, mods, lat_row, gains, layer)
        if not last:
            four_ctx = _fourier(f_ctx, w_four_b, layer, lc, None)
            ctx, h_ctx = _outproj(four_ctx, flat(rec_ctx), w_out_b, ctx, mods, ctx_row, gains, layer)
            ctx = _ffn(h_ctx, w_ffn_in_b, w_ffn_out_b, ctx, mods, ctx_row, gains, layer)
    return x
```

```python
import functools
import math

import numpy as np
import jax
import jax.numpy as jnp
from jax import lax
from jax.experimental import pallas as pl
from jax.experimental.pallas import tpu as pltpu

GRID_W = 64
CONV_LEFT = 2
LRU_C = 8.0
EPS = 1e-6
N_MOD = 6

F32 = jnp.float32
BF16 = jnp.bfloat16

V7X_VMEM_LIMIT_BYTES = 56 * 1024 * 1024
SUBLANES = 8
LANES = 128
ROW_PITCH_PAD = 8
FOURIER_COL_UNROLL = 32
FOURIER_Q_BATCH = 32
IN_ROW_TILE = 1024
ROW_TILE = 512
PROJ_SUB_ROWS = 256
OUT_SUB_ROWS = 128
FFN_ROW_TILE = 1024
FFN_SUB_ROWS = 512
FFN_COL_TILE = 512
FFN_OUT_COLS = 512
SCAN_CHUNK = 256
SCAN_SEQS = 4
SCAN_UNROLL = 16
SQRT_GUARD = 1e-30


def _params(*sem, vmem=V7X_VMEM_LIMIT_BYTES):
    return pltpu.CompilerParams(dimension_semantics=sem, vmem_limit_bytes=vmem)


def _dot(a, b):
    return jnp.dot(a, b, preferred_element_type=F32)


def _tile(n, pref):
    t = min(n, pref)
    while n % t:
        t //= 2
    return t


def _rms(x, gain):
    ms = jnp.mean(x * x, axis=-1, keepdims=True)
    return x * lax.rsqrt(ms + EPS) * gain


def _resident(shape):
    nd = len(shape)
    return pl.BlockSpec(shape, lambda *_: (0,) * nd, pipeline_mode=pl.Buffered(1))


def _layer_block(arr, *lead):
    nlead = len(lead)
    rest = arr.shape[nlead:]
    idx = tuple(lead) + (0,) * len(rest)
    return pl.BlockSpec((None,) * nlead + tuple(rest), lambda *_: idx, pipeline_mode=pl.Buffered(1))


def _mod_spec(mods, layer, row_of):
    return pl.BlockSpec((None, None, 1, mods.shape[-1]), lambda bi, *_: (layer, row_of(bi), 0, 0))


def _mod(mod_ref, k):
    d = mod_ref.shape[-1] // N_MOD
    return mod_ref[:, k * d:(k + 1) * d]


def _ada_kernel(c_ref, w_ref, b_ref, o_ref):
    c = c_ref[...]
    s = (c * jax.nn.sigmoid(c)).astype(BF16)
    o_ref[...] = _dot(s, w_ref[...].astype(BF16)) + b_ref[...]


def _ada(cs, w_ada, b_ada):
    depth, d, nm = w_ada.shape
    rows = cs.shape[0]
    tn = _tile(nm, 1024)
    return pl.pallas_call(
        _ada_kernel,
        grid=(depth, nm // tn),
        in_specs=[
            pl.BlockSpec((rows, d), lambda l, j: (0, 0)),
            pl.BlockSpec((None, d, tn), lambda l, j: (l, 0, j)),
            pl.BlockSpec((None, 1, tn), lambda l, j: (l, 0, j)),
        ],
        out_specs=pl.BlockSpec((None, rows, tn), lambda l, j: (l, 0, j)),
        out_shape=jax.ShapeDtypeStruct((depth, rows, nm), F32),
        compiler_params=_params("arbitrary", "arbitrary"),
        name="ada_mod",
    )(cs, w_ada, b_ada.reshape(depth, 1, nm))


def _inproj_kernel(x_ref, mod_ref, gain_ref, w_ref, f_ref, r_ref, g_ref):
    tm = x_ref.shape[0]
    nf = f_ref.shape[-1]
    nr = g_ref.shape[-1]
    sub = min(tm, PROJ_SUB_ROWS)
    gain = gain_ref[0] * (1.0 + _mod(mod_ref, 1))
    shift = _mod(mod_ref, 0)
    for s in range(tm // sub):
        rows = slice(s * sub, (s + 1) * sub)
        h = (_rms(x_ref[rows, :], gain) + shift).astype(BF16)
        g_ref[rows, :] = jax.nn.gelu(_dot(h, w_ref[:, nf + nr:])).astype(g_ref.dtype)
        r_ref[rows, :] = _dot(h, w_ref[:, nf:nf + nr])
        f_ref[rows, :] = _dot(h, w_ref[:, :nf]).astype(f_ref.dtype)


def _inproj(x, mods, row_of, gains, w_in, layer, fw, lw):
    b, l, d = x.shape
    tm = _tile(l, IN_ROW_TILE)
    row = lambda bi, i: (bi, i, 0)
    return pl.pallas_call(
        _inproj_kernel,
        grid=(b, l // tm),
        in_specs=[pl.BlockSpec((None, tm, d), row), _mod_spec(mods, layer, row_of),
                  _layer_block(gains, layer), _layer_block(w_in, layer)],
        out_specs=[pl.BlockSpec((None, tm, fw), row), pl.BlockSpec((None, tm, lw), row),
                   pl.BlockSpec((None, tm, lw), row)],
        out_shape=[jax.ShapeDtypeStruct((b, l, fw), BF16), jax.ShapeDtypeStruct((b, l, lw), F32),
                   jax.ShapeDtypeStruct((b, l, lw), BF16)],
        compiler_params=_params("arbitrary", "arbitrary"),
        name="in_proj",
    )(x, mods, gains, w_in)


def _dft(n):
    k = np.arange(n)
    ang = 2.0 * np.pi * ((k[:, None] * k[None, :]) % n) / n
    return np.cos(ang) / math.sqrt(n), np.sin(ang) / math.sqrt(n)


def _fold_channel_dft(ck_ref, sk_ref, wf_ref):
    wf = wf_ref[...]
    return _dot(ck_ref[...], wf).astype(BF16), (-_dot(sk_ref[...], wf)).astype(BF16)


def _fourier_grid_kernel(u_ref, mc_ref, mr_ref, ck_ref, sk_ref, wf_ref, o_ref, pc_ref, ps_ref, *,
                         rows, cols, pitch):
    gd = u_ref.shape[-1]
    nslab = gd // LANES
    qb = math.gcd(cols, FOURIER_Q_BATCH)
    wa, wb = _fold_channel_dft(ck_ref, sk_ref, wf_ref)
    wab = jnp.concatenate([wa, wb], axis=0)
    mc = mc_ref[...]
    mr = mr_ref[...]

    def col_stage(r, carry):
        src = pl.multiple_of(r * cols, cols)
        dst = pl.multiple_of(r * pitch, SUBLANES)
        t = _dot(mc, u_ref[pl.ds(src, cols), :])
        for s in range(nslab):
            lanes = slice(s * LANES, (s + 1) * LANES)
            pc_ref[s, pl.ds(dst, cols), :] = t[:cols, lanes]
            ps_ref[s, pl.ds(dst, cols), :] = t[cols:, lanes]
        return carry

    lax.fori_loop(0, rows, col_stage, 0, unroll=FOURIER_COL_UNROLL)

    def gather(ref, q):
        return jnp.concatenate([ref.at[s][pl.ds(q, rows, stride=pitch), :] for s in range(nslab)],
                               axis=-1).astype(BF16)

    def row_stage(it, carry):
        q0 = it * qb
        x = jnp.concatenate(
            [jnp.concatenate([gather(pc_ref, q0 + k), gather(ps_ref, q0 + k)], axis=0)
             for k in range(qb)], axis=-1)
        y = _dot(mr, x).astype(BF16)
        lhs = jnp.concatenate(
            [jnp.concatenate([y[:rows, k * gd:(k + 1) * gd], y[rows:, k * gd:(k + 1) * gd]], axis=-1)
             for k in range(qb)], axis=0)
        z = _dot(lhs, wab)
        for k in range(qb):
            for s in range(nslab):
                pc_ref.at[s][pl.ds(q0 + k, rows, stride=pitch), :] = (
                    z[k * rows:(k + 1) * rows, s * LANES:(s + 1) * LANES])
        return carry

    lax.fori_loop(0, cols // qb, row_stage, 0)

    def emit(r, carry):
        src = pl.multiple_of(r * pitch, SUBLANES)
        dst = pl.multiple_of(r * cols, cols)
        for s in range(nslab):
            o_ref[pl.ds(dst, cols), s * LANES:(s + 1) * LANES] = (
                pc_ref[s, pl.ds(src, cols), :].astype(o_ref.dtype))
        return carry

    lax.fori_loop(0, rows, emit, 0, unroll=FOURIER_COL_UNROLL)


def _fourier_seq_kernel(u_ref, ml_ref, ck_ref, sk_ref, wf_ref, o_ref):
    l = u_ref.shape[0]
    wa, wb = _fold_channel_dft(ck_ref, sk_ref, wf_ref)
    y = _dot(ml_ref[...], u_ref[...])
    z = _dot(y[:l].astype(BF16), wa) + _dot(y[l:].astype(BF16), wb)
    o_ref[...] = z.astype(o_ref.dtype)


def _fourier(u, w_four, layer, grid_w):
    b, l, fw = u.shape
    _, g, gd, _ = w_four.shape
    ck, sk = _dft(gd)
    ck, sk = jnp.asarray(ck, BF16), jnp.asarray(sk, BF16)
    blk = pl.BlockSpec((None, l, gd), lambda bi, gi: (bi, 0, gi))
    wspec = pl.BlockSpec((None, None, gd, gd), lambda bi, gi: (layer, gi, 0, 0))
    common = dict(
        grid=(b, g),
        out_specs=blk,
        out_shape=jax.ShapeDtypeStruct((b, l, fw), BF16),
        compiler_params=_params("arbitrary", "arbitrary"),
    )
    if grid_w is None:
        cl, sl = _dft(l)
        ml = jnp.asarray(np.concatenate([cl, sl], axis=0), BF16)
        return pl.pallas_call(
            _fourier_seq_kernel,
            in_specs=[blk, _resident(ml.shape), _resident(ck.shape), _resident(sk.shape), wspec],
            name="fourier_seq", **common,
        )(u, ml, ck, sk, w_four)
    cols = grid_w
    rows = l // cols
    pitch = cols + ROW_PITCH_PAD
    cc, sc = _dft(cols)
    cr, sr = _dft(rows)
    mc = jnp.asarray(np.concatenate([cc, sc], axis=0), BF16)
    mr = jnp.asarray(np.block([[cr, -sr], [sr, cr]]), BF16)
    scratch = pltpu.VMEM((gd // LANES, rows * pitch, LANES), F32)
    return pl.pallas_call(
        functools.partial(_fourier_grid_kernel, rows=rows, cols=cols, pitch=pitch),
        in_specs=[blk, _resident(mc.shape), _resident(mr.shape), _resident(ck.shape),
                  _resident(sk.shape), wspec],
        scratch_shapes=[scratch, scratch],
        name="fourier_grid", **common,
    )(u, mc, mr, ck, sk, w_four)


def _from_time_major(src, t, j):
    return src[pl.ds(j, t, stride=SUBLANES), :]


def _rglru_kernel(*refs, reverse, conv, gated, heads):
    refs = list(refs)
    if conv:
        r_ref, rp_ref, rn_ref, cw_ref, cb_ref = refs[:5]
        del refs[:5]
    else:
        u_ref = refs.pop(0)
    wg_ref, bg_ref, lam_ref, h0_ref = refs[:4]
    del refs[:4]
    if gated:
        other_ref, g_ref = refs[:2]
        del refs[:2]
    o_ref = refs.pop(0)
    if conv:
        uo_ref = refs.pop(0)
    hfin_ref = refs.pop(0)
    if conv:
        rt_sc, ut_sc = refs[:2]
        del refs[:2]
    at_sc, bt_sc, ht_sc, carry = refs

    nbat, t, w = o_ref.shape
    hd = w // heads
    ncol = w // LANES
    i = pl.program_id(1)
    n = pl.num_programs(1)
    ci = (n - 1 - i) if reverse else i

    @pl.when(i == 0)
    def _():
        carry[...] = h0_ref[...]

    half_clam = (-0.5 * LRU_C * math.log2(math.e)) * jax.nn.softplus(-lam_ref[...])

    for bi in range(nbat):
        if conv:
            rt = rt_sc.at[bi]
            prev = jnp.where(ci == 0, 0.0, rp_ref[bi])
            nxt = jnp.where(ci == n - 1, 0.0, rn_ref[bi])
            for j in range(ncol):
                lanes = slice(j * LANES, (j + 1) * LANES)
                rt[pl.ds(j, SUBLANES, stride=SUBLANES), :] = prev[:, lanes]
                rt[pl.ds(SUBLANES * SUBLANES + j, t, stride=SUBLANES), :] = r_ref[bi, :, lanes]
                rt[pl.ds((SUBLANES + t) * SUBLANES + j, SUBLANES, stride=SUBLANES), :] = nxt[:, lanes]
            ut = cb_ref[...][None]
            for k in range(cw_ref.shape[0]):
                off = (SUBLANES - CONV_LEFT + k) * SUBLANES
                tap = rt_sc[bi, off:off + t * SUBLANES, :]
                ut = ut + cw_ref[k][None] * tap.reshape(t, SUBLANES, LANES)
            ut_sc[bi] = ut.reshape(t * SUBLANES, LANES)
            u = jnp.concatenate([_from_time_major(ut_sc.at[bi], t, j) for j in range(ncol)], axis=-1)
            uo_ref[bi] = u
        else:
            u = u_ref[bi]
        ub = u.astype(BF16)

        for h in range(heads):
            ch = slice(h * hd, (h + 1) * hd)
            gates = _dot(ub[:, ch], wg_ref[h])
            tr = jnp.tanh(gates[:, :hd] + bg_ref[0:1, ch])
            ti = jnp.tanh(gates[:, hd:] + bg_ref[1:2, ch])
            a = jnp.exp2(half_clam[:, ch] + half_clam[:, ch] * tr)
            m = jnp.maximum((1.0 - a) * (1.0 + a), 0.0)
            bx = (m * lax.rsqrt(jnp.maximum(m, SQRT_GUARD))) * ((0.5 + 0.5 * ti) * u[:, ch])
            for jj in range(hd // LANES):
                j = h * (hd // LANES) + jj
                lanes = slice(jj * LANES, (jj + 1) * LANES)
                at_sc.at[bi][pl.ds(j, t, stride=SUBLANES), :] = a[:, lanes]
                bt_sc.at[bi][pl.ds(j, t, stride=SUBLANES), :] = bx[:, lanes]

    grp = math.gcd(t, SCAN_UNROLL)

    def steps(gi, hs):
        gpos = (t // grp - 1 - gi) if reverse else gi
        base = pl.multiple_of(gpos * grp * SUBLANES, grp * SUBLANES)
        hs = list(hs)
        for k in range(grp):
            rows = pl.ds(base + ((grp - 1 - k) if reverse else k) * SUBLANES, SUBLANES)
            for bi in range(nbat):
                hs[bi] = at_sc[bi, rows, :] * hs[bi] + bt_sc[bi, rows, :]
                ht_sc[bi, rows, :] = hs[bi]
        return tuple(hs)

    hfin = lax.fori_loop(0, t // grp, steps, tuple(carry[bi] for bi in range(nbat)))
    for bi in range(nbat):
        carry[bi] = hfin[bi]
        hfin_ref[bi] = hfin[bi]
        for j in range(ncol):
            lanes = slice(j * LANES, (j + 1) * LANES)
            rec = _from_time_major(ht_sc.at[bi], t, j)
            if gated:
                o_ref[bi, :, lanes] = (g_ref[bi, :, lanes].astype(F32)
                                       * (other_ref[bi, :, lanes] + rec)).astype(o_ref.dtype)
            else:
                o_ref[bi, :, lanes] = rec


def _rglru(src, layer, direction, params, h0, other=None, gate=None):
    conv_w, conv_b, w_gate, b_gate, lam = params
    reverse = bool(direction)
    conv = not reverse
    gated = other is not None
    heads = w_gate.shape[2]
    b, l, w = src.shape
    assert w == SUBLANES * LANES, "time-major layout packs the recurrent width into one (8, 128) tile"
    nbat = max(k for k in (1, 2, SCAN_SEQS) if b % k == 0)
    t = _tile(l, SCAN_CHUNK)
    n = l // t
    tb = t // SUBLANES
    nb = l // SUBLANES

    def pos(i):
        return (n - 1 - i) if reverse else i

    main = pl.BlockSpec((nbat, t, w), lambda bi, i: (bi, pos(i), 0))
    state = pl.BlockSpec((nbat, SUBLANES, LANES), lambda bi, i: (bi, 0, 0))
    in_specs, args = [main], [src]
    if conv:
        in_specs += [
            pl.BlockSpec((nbat, SUBLANES, w), lambda bi, i: (bi, jnp.maximum(pos(i) * tb - 1, 0), 0)),
            pl.BlockSpec((nbat, SUBLANES, w), lambda bi, i: (bi, jnp.minimum((pos(i) + 1) * tb, nb - 1), 0)),
            _layer_block(conv_w, layer), _layer_block(conv_b, layer)]
        args += [src, src, conv_w, conv_b]
    in_specs += [_layer_block(w_gate, layer, direction), _layer_block(b_gate, layer, direction),
                 _layer_block(lam, layer, direction), state]
    args += [w_gate, b_gate, lam, h0]
    if gated:
        in_specs += [main, main]
        args += [other, gate]
    out_specs = [main] + ([main] if conv else []) + [state]
    out_shape = ([jax.ShapeDtypeStruct((b, l, w), BF16 if gated else F32)]
                 + ([jax.ShapeDtypeStruct((b, l, w), F32)] if conv else [])
                 + [jax.ShapeDtypeStruct((b, SUBLANES, LANES), F32)])
    tmaj = pltpu.VMEM((nbat, t * SUBLANES, LANES), F32)
    halo = pltpu.VMEM((nbat, (t + 2 * SUBLANES) * SUBLANES, LANES), F32)
    scratch = ([halo, tmaj] if conv else []) + [tmaj, tmaj, tmaj, pltpu.VMEM((nbat, SUBLANES, LANES), F32)]
    return pl.pallas_call(
        functools.partial(_rglru_kernel, reverse=reverse, conv=conv, gated=gated, heads=heads),
        grid=(b // nbat, n),
        in_specs=in_specs,
        out_specs=out_specs,
        out_shape=out_shape,
        scratch_shapes=scratch,
        compiler_params=_params("arbitrary", "arbitrary"),
        name="rglru_bwd" if reverse else "rglru_fwd",
    )(*args)


def _outproj_kernel(four_ref, rec_ref, w_ref, x_ref, mod_ref, gain_ref, xo_ref, h_ref):
    tm = x_ref.shape[0]
    sub = min(tm, OUT_SUB_ROWS)
    post = gain_ref[1] * _mod(mod_ref, 2)
    pre = gain_ref[2] * (1.0 + _mod(mod_ref, 4))
    shift = _mod(mod_ref, 3)
    for s in range(tm // sub):
        rows = slice(s * sub, (s + 1) * sub)
        lhs = jnp.concatenate([four_ref[rows, :], rec_ref[rows, :]], axis=-1)
        y = _dot(lhs, w_ref[...])
        xn = x_ref[rows, :] + _rms(y, post)
        xo_ref[rows, :] = xn
        h_ref[rows, :] = (_rms(xn, pre) + shift).astype(h_ref.dtype)


def _outproj(four, rec, w_out, x, mods, row_of, gains, layer):
    b, l, d = x.shape
    fw, lw = four.shape[-1], rec.shape[-1]
    tm = _tile(l, ROW_TILE)
    row = lambda bi, i: (bi, i, 0)
    return pl.pallas_call(
        _outproj_kernel,
        grid=(b, l // tm),
        in_specs=[pl.BlockSpec((None, tm, fw), row), pl.BlockSpec((None, tm, lw), row),
                  _layer_block(w_out, layer), pl.BlockSpec((None, tm, d), row),
                  _mod_spec(mods, layer, row_of), _layer_block(gains, layer)],
        out_specs=[pl.BlockSpec((None, tm, d), row), pl.BlockSpec((None, tm, d), row)],
        out_shape=[jax.ShapeDtypeStruct((b, l, d), F32), jax.ShapeDtypeStruct((b, l, d), BF16)],
        compiler_params=_params("arbitrary", "arbitrary"),
        name="out_proj",
    )(four, rec, w_out, x, mods, gains)


def _ffn_kernel(h_ref, wg_ref, wu_ref, wo_ref, x_ref, mod_ref, gain_ref, o_ref, *, nj):
    j = pl.program_id(2)
    tm, d = o_ref.shape
    sub = min(tm, FFN_SUB_ROWS)
    post = gain_ref[3] * _mod(mod_ref, 5)

    def chunk(first, last):
        for s in range(tm // sub):
            rows = slice(s * sub, (s + 1) * sub)
            h = h_ref[rows, :]
            g = _dot(h, wg_ref[...])
            u = _dot(h, wu_ref[...])
            act = (g * jax.nn.sigmoid(g) * u).astype(BF16)
            for c in range(0, d, FFN_OUT_COLS):
                cols = slice(c, c + FFN_OUT_COLS)
                part = _dot(act, wo_ref[:, cols])
                o_ref[rows, cols] = part if first else o_ref[rows, cols] + part
            if last:
                o_ref[rows, :] = x_ref[rows, :] + _rms(o_ref[rows, :], post)

    if nj == 1:
        chunk(True, True)
    else:
        pl.when(j == 0)(functools.partial(chunk, True, False))
        pl.when((j > 0) & (j < nj - 1))(functools.partial(chunk, False, False))
        pl.when(j == nj - 1)(functools.partial(chunk, False, True))


def _ffn(h, w_ffn_in, w_ffn_out, x, mods, row_of, gains, layer):
    b, l, d = x.shape
    dff = w_ffn_out.shape[1]
    tm = _tile(l, FFN_ROW_TILE)
    tf = _tile(dff, FFN_COL_TILE)
    nj = dff // tf
    row = lambda bi, i, j: (bi, i, 0)
    return pl.pallas_call(
        functools.partial(_ffn_kernel, nj=nj),
        grid=(b, l // tm, nj),
        in_specs=[pl.BlockSpec((None, tm, d), row),
                  pl.BlockSpec((None, d, tf), lambda bi, i, j: (layer, 0, j)),
                  pl.BlockSpec((None, d, tf), lambda bi, i, j: (layer, 0, nj + j)),
                  pl.BlockSpec((None, tf, d), lambda bi, i, j: (layer, j, 0)),
                  pl.BlockSpec((None, tm, d), row),
                  _mod_spec(mods, layer, row_of), _layer_block(gains, layer)],
        out_specs=pl.BlockSpec((None, tm, d), row),
        out_shape=jax.ShapeDtypeStruct((b, l, d), F32),
        compiler_params=_params("arbitrary", "arbitrary", "arbitrary", vmem=60 * 1024 * 1024),
        name="ffn",
    )(h, w_ffn_in, w_ffn_in, w_ffn_out, x, mods, gains)


def kernel(x, c, ctx, c_ctx, w_ada, b_ada, norm_g, w_in, w_four, conv_w, conv_b, lru_w, lru_b, lru_lam,
           w_out, w_ffn_in, w_ffn_out):
    depth = w_ada.shape[0]
    bsz, _, d = x.shape
    lc = ctx.shape[1]
    fw = w_four.shape[1] * w_four.shape[2]
    lw = conv_w.shape[-1]

    pad = (-(bsz + 1)) % SUBLANES
    cs = jnp.concatenate([c, c_ctx[None, :], jnp.zeros((pad, d), F32)], axis=0)
    mods = _ada(cs, w_ada, b_ada)
    mods = mods.reshape(depth, mods.shape[1], 1, N_MOD * d)
    lat_row = lambda bi: bi
    ctx_row = lambda bi: bsz

    gains = norm_g.reshape(depth, norm_g.shape[1], 1, d)
    w_in_b = w_in.astype(BF16)
    w_out_b = w_out.astype(BF16)
    w_ffn_in_b = w_ffn_in.astype(BF16)
    w_ffn_out_b = w_ffn_out.astype(BF16)
    w_four_b = w_four.astype(BF16)
    w_gate_b = (0.5 * jnp.concatenate([lru_w[:, :, 0], lru_w[:, :, 1]], axis=-1)).astype(BF16)
    scan_params = (conv_w.reshape(depth, conv_w.shape[1], SUBLANES, LANES),
                   conv_b.reshape(depth, SUBLANES, LANES), w_gate_b, 0.5 * lru_b,
                   lru_lam.reshape(depth, 2, 1, lw))
    zero_state = jnp.zeros((bsz, SUBLANES, LANES), F32)

    ctx = ctx.reshape(1, bsz * lc, d)
    per_seq = lambda a: a.reshape(bsz, -1, a.shape[-1])
    flat = lambda a: a.reshape(1, bsz * lc, a.shape[-1])

    for layer in range(depth):
        last = layer == depth - 1
        f_lat, r_lat, g_lat = _inproj(x, mods, lat_row, gains, w_in_b, layer, fw, lw)
        f_ctx, r_ctx, g_ctx = _inproj(ctx, mods, ctx_row, gains, w_in_b, layer, fw, lw)

        recf_ctx, u_ctx, hf = _rglru(per_seq(r_ctx), layer, 0, scan_params, zero_state)
        if last:
            _, hb = _rglru(u_ctx, layer, 1, scan_params, zero_state)
        else:
            rec_ctx, hb = _rglru(u_ctx, layer, 1, scan_params, zero_state, recf_ctx, per_seq(g_ctx))
        recf_lat, u_lat, _ = _rglru(r_lat, layer, 0, scan_params, hf)
        rec_lat, _ = _rglru(u_lat, layer, 1, scan_params, hb, recf_lat, g_lat)

        four_lat = _fourier(f_lat, w_four_b, layer, GRID_W)
        x, h_lat = _outproj(four_lat, rec_lat, w_out_b, x, mods, lat_row, gains, layer)
        x = _ffn(h_lat, w_ffn_in_b, w_ffn_out_b, x, mods, lat_row, gains, layer)
        if not last:
            four_ctx = _fourier(per_seq(f_ctx), w_four_b, layer, None)
            ctx, h_ctx = _outproj(flat(four_ctx), flat(rec_ctx), w_out_b, ctx, mods, ctx_row, gains, layer)
            ctx = _ffn(h_ctx, w_ffn_in_b, w_ffn_out_b, ctx, mods, ctx_row, gains, layer)
    return x
```

```python
import functools
import math

import numpy as np
import jax
import jax.numpy as jnp
from jax import lax
from jax.experimental import pallas as pl
from jax.experimental.pallas import tpu as pltpu

GRID_W = 64
CONV_LEFT = 2
LRU_C = 8.0
EPS = 1e-6
N_MOD = 6

F32 = jnp.float32
BF16 = jnp.bfloat16

V7X_VMEM_LIMIT_BYTES = 56 * 1024 * 1024
SUBLANES = 8
LANES = 128
ROW_PITCH_PAD = 8
FOURIER_COL_UNROLL = 32
FOURIER_Q_BATCH = 32
IN_ROW_TILE = 1024
ROW_TILE = 512
PROJ_SUB_ROWS = 256
OUT_SUB_ROWS = 128
FFN_ROW_TILE = 1024
FFN_SUB_ROWS = 512
FFN_COL_TILE = 512
FFN_OUT_COLS = 512
SCAN_CHUNK = 256
SCAN_SEQS = 4
SCAN_UNROLL = 16
SQRT_GUARD = 1e-30


def _params(*sem, vmem=V7X_VMEM_LIMIT_BYTES):
    return pltpu.CompilerParams(dimension_semantics=sem, vmem_limit_bytes=vmem)


def _dot(a, b):
    return jnp.dot(a, b, preferred_element_type=F32)


def _tile(n, pref):
    t = min(n, pref)
    while n % t:
        t //= 2
    return t


def _rms(x, gain):
    ms = jnp.mean(x * x, axis=-1, keepdims=True)
    return x * lax.rsqrt(ms + EPS) * gain


def _resident(shape):
    nd = len(shape)
    return pl.BlockSpec(shape, lambda *_: (0,) * nd, pipeline_mode=pl.Buffered(1))


def _layer_block(arr, *lead):
    nlead = len(lead)
    rest = arr.shape[nlead:]
    idx = tuple(lead) + (0,) * len(rest)
    return pl.BlockSpec((None,) * nlead + tuple(rest), lambda *_: idx, pipeline_mode=pl.Buffered(1))


def _mod_spec(mods, layer, row_of):
    return pl.BlockSpec((None, None, 1, mods.shape[-1]), lambda bi, *_: (layer, row_of(bi), 0, 0))


def _mod(mod_ref, k):
    d = mod_ref.shape[-1] // N_MOD
    return mod_ref[:, k * d:(k + 1) * d]


def _ada_kernel(c_ref, w_ref, b_ref, o_ref):
    c = c_ref[...]
    s = (c * jax.nn.sigmoid(c)).astype(BF16)
    o_ref[...] = _dot(s, w_ref[...].astype(BF16)) + b_ref[...]


def _ada(cs, w_ada, b_ada):
    depth, d, nm = w_ada.shape
    rows = cs.shape[0]
    tn = _tile(nm, 1024)
    return pl.pallas_call(
        _ada_kernel,
        grid=(depth, nm // tn),
        in_specs=[
            pl.BlockSpec((rows, d), lambda l, j: (0, 0)),
            pl.BlockSpec((None, d, tn), lambda l, j: (l, 0, j)),
            pl.BlockSpec((None, 1, tn), lambda l, j: (l, 0, j)),
        ],
        out_specs=pl.BlockSpec((None, rows, tn), lambda l, j: (l, 0, j)),
        out_shape=jax.ShapeDtypeStruct((depth, rows, nm), F32),
        compiler_params=_params("arbitrary", "arbitrary"),
        name="ada_mod",
    )(cs, w_ada, b_ada.reshape(depth, 1, nm))


def _inproj_kernel(x_ref, mod_ref, gain_ref, w_ref, f_ref, r_ref, g_ref):
    tm = x_ref.shape[0]
    nf = f_ref.shape[-1]
    nr = g_ref.shape[-1]
    sub = min(tm, PROJ_SUB_ROWS)
    gain = gain_ref[0] * (1.0 + _mod(mod_ref, 1))
    shift = _mod(mod_ref, 0)
    for s in range(tm // sub):
        rows = slice(s * sub, (s + 1) * sub)
        h = (_rms(x_ref[rows, :], gain) + shift).astype(BF16)
        g_ref[rows, :] = jax.nn.gelu(_dot(h, w_ref[:, nf + nr:])).astype(g_ref.dtype)
        r_ref[rows, :] = _dot(h, w_ref[:, nf:nf + nr])
        f_ref[rows, :] = _dot(h, w_ref[:, :nf]).astype(f_ref.dtype)


def _inproj(x, mods, row_of, gains, w_in, layer, fw, lw):
    b, l, d = x.shape
    tm = _tile(l, IN_ROW_TILE)
    row = lambda bi, i: (bi, i, 0)
    return pl.pallas_call(
        _inproj_kernel,
        grid=(b, l // tm),
        in_specs=[pl.BlockSpec((None, tm, d), row), _mod_spec(mods, layer, row_of),
                  _layer_block(gains, layer), _layer_block(w_in, layer)],
        out_specs=[pl.BlockSpec((None, tm, fw), row), pl.BlockSpec((None, tm, lw), row),
                   pl.BlockSpec((None, tm, lw), row)],
        out_shape=[jax.ShapeDtypeStruct((b, l, fw), BF16), jax.ShapeDtypeStruct((b, l, lw), F32),
                   jax.ShapeDtypeStruct((b, l, lw), BF16)],
        compiler_params=_params("arbitrary", "arbitrary"),
        name="in_proj",
    )(x, mods, gains, w_in)


def _dft(n):
    k = np.arange(n)
    ang = 2.0 * np.pi * ((k[:, None] * k[None, :]) % n) / n
    return np.cos(ang) / math.sqrt(n), np.sin(ang) / math.sqrt(n)


def _fold_channel_dft(ck_ref, sk_ref, wf_ref):
    wf = wf_ref[...]
    return _dot(ck_ref[...], wf).astype(BF16), (-_dot(sk_ref[...], wf)).astype(BF16)


def _fourier_grid_kernel(u_ref, mc_ref, mr_ref, ck_ref, sk_ref, wf_ref, o_ref, pc_ref, ps_ref, *,
                         rows, cols, pitch):
    gd = u_ref.shape[-1]
    nslab = gd // LANES
    qb = math.gcd(cols, FOURIER_Q_BATCH)
    wa, wb = _fold_channel_dft(ck_ref, sk_ref, wf_ref)
    wab = jnp.concatenate([wa, wb], axis=0)
    mc = mc_ref[...]
    mr = mr_ref[...]

    def col_stage(r, carry):
        src = pl.multiple_of(r * cols, cols)
        dst = pl.multiple_of(r * pitch, SUBLANES)
        t = _dot(mc, u_ref[pl.ds(src, cols), :])
        for s in range(nslab):
            lanes = slice(s * LANES, (s + 1) * LANES)
            pc_ref[s, pl.ds(dst, cols), :] = t[:cols, lanes]
            ps_ref[s, pl.ds(dst, cols), :] = t[cols:, lanes]
        return carry

    lax.fori_loop(0, rows, col_stage, 0, unroll=FOURIER_COL_UNROLL)

    def gather(ref, q):
        return jnp.concatenate([ref.at[s][pl.ds(q, rows, stride=pitch), :] for s in range(nslab)],
                               axis=-1).astype(BF16)

    def row_stage(it, carry):
        q0 = it * qb
        x = jnp.concatenate(
            [jnp.concatenate([gather(pc_ref, q0 + k), gather(ps_ref, q0 + k)], axis=0)
             for k in range(qb)], axis=-1)
        y = _dot(mr, x).astype(BF16)
        lhs = jnp.concatenate(
            [jnp.concatenate([y[:rows, k * gd:(k + 1) * gd], y[rows:, k * gd:(k + 1) * gd]], axis=-1)
             for k in range(qb)], axis=0)
        z = _dot(lhs, wab)
        for k in range(qb):
            for s in range(nslab):
                pc_ref.at[s][pl.ds(q0 + k, rows, stride=pitch), :] = (
                    z[k * rows:(k + 1) * rows, s * LANES:(s + 1) * LANES])
        return carry

    lax.fori_loop(0, cols // qb, row_stage, 0)

    def emit(r, carry):
        src = pl.multiple_of(r * pitch, SUBLANES)
        dst = pl.multiple_of(r * cols, cols)
        for s in range(nslab):
            o_ref[pl.ds(dst, cols), s * LANES:(s + 1) * LANES] = (
                pc_ref[s, pl.ds(src, cols), :].astype(o_ref.dtype))
        return carry

    lax.fori_loop(0, rows, emit, 0, unroll=FOURIER_COL_UNROLL)


def _fourier_seq_kernel(u_ref, ml_ref, ck_ref, sk_ref, wf_ref, o_ref):
    l = u_ref.shape[0]
    wa, wb = _fold_channel_dft(ck_ref, sk_ref, wf_ref)
    y = _dot(ml_ref[...], u_ref[...])
    z = _dot(y[:l].astype(BF16), wa) + _dot(y[l:].astype(BF16), wb)
    o_ref[...] = z.astype(o_ref.dtype)


def _fourier(u, w_four, layer, grid_w):
    b, l, fw = u.shape
    _, g, gd, _ = w_four.shape
    ck, sk = _dft(gd)
    ck, sk = jnp.asarray(ck, BF16), jnp.asarray(sk, BF16)
    blk = pl.BlockSpec((None, l, gd), lambda bi, gi: (bi, 0, gi))
    wspec = pl.BlockSpec((None, None, gd, gd), lambda bi, gi: (layer, gi, 0, 0))
    common = dict(
        grid=(b, g),
        out_specs=blk,
        out_shape=jax.ShapeDtypeStruct((b, l, fw), BF16),
        compiler_params=_params("arbitrary", "arbitrary"),
    )
    if grid_w is None:
        cl, sl = _dft(l)
        ml = jnp.asarray(np.concatenate([cl, sl], axis=0), BF16)
        return pl.pallas_call(
            _fourier_seq_kernel,
            in_specs=[blk, _resident(ml.shape), _resident(ck.shape), _resident(sk.shape), wspec],
            name="fourier_seq", **common,
        )(u, ml, ck, sk, w_four)
    cols = grid_w
    rows = l // cols
    pitch = cols + ROW_PITCH_PAD
    cc, sc = _dft(cols)
    cr, sr = _dft(rows)
    mc = jnp.asarray(np.concatenate([cc, sc], axis=0), BF16)
    mr = jnp.asarray(np.block([[cr, -sr], [sr, cr]]), BF16)
    scratch = pltpu.VMEM((gd // LANES, rows * pitch, LANES), F32)
    return pl.pallas_call(
        functools.partial(_fourier_grid_kernel, rows=rows, cols=cols, pitch=pitch),
        in_specs=[blk, _resident(mc.shape), _resident(mr.shape), _resident(ck.shape),
                  _resident(sk.shape), wspec],
        scratch_shapes=[scratch, scratch],
        name="fourier_grid", **common,
    )(u, mc, mr, ck, sk, w_four)


def _from_time_major(src, t, j):
    return src[pl.ds(j, t, stride=SUBLANES), :]


def _rglru_kernel(*refs, reverse, conv, gated, heads):
    refs = list(refs)
    if conv:
        r_ref, rp_ref, rn_ref, cw_ref, cb_ref = refs[:5]
        del refs[:5]
    else:
        u_ref = refs.pop(0)
    wg_ref, bg_ref, lam_ref, h0_ref = refs[:4]
    del refs[:4]
    if gated:
        other_ref, g_ref = refs[:2]
        del refs[:2]
    o_ref = refs.pop(0)
    if conv:
        uo_ref = refs.pop(0)
    hfin_ref = refs.pop(0)
    if conv:
        rt_sc, ut_sc = refs[:2]
        del refs[:2]
    at_sc, bt_sc, ht_sc, carry = refs

    nbat, t, w = o_ref.shape
    hd = w // heads
    ncol = w // LANES
    i = pl.program_id(1)
    n = pl.num_programs(1)
    ci = (n - 1 - i) if reverse else i

    @pl.when(i == 0)
    def _():
        carry[...] = h0_ref[...]

    half_clam = (-0.5 * LRU_C * math.log2(math.e)) * jax.nn.softplus(-lam_ref[...])

    for bi in range(nbat):
        if conv:
            rt = rt_sc.at[bi]
            prev = jnp.where(ci == 0, 0.0, rp_ref[bi])
            nxt = jnp.where(ci == n - 1, 0.0, rn_ref[bi])
            for j in range(ncol):
                lanes = slice(j * LANES, (j + 1) * LANES)
                rt[pl.ds(j, SUBLANES, stride=SUBLANES), :] = prev[:, lanes]
                rt[pl.ds(SUBLANES * SUBLANES + j, t, stride=SUBLANES), :] = r_ref[bi, :, lanes]
                rt[pl.ds((SUBLANES + t) * SUBLANES + j, SUBLANES, stride=SUBLANES), :] = nxt[:, lanes]
            ut = cb_ref[...][None]
            for k in range(cw_ref.shape[0]):
                off = (SUBLANES - CONV_LEFT + k) * SUBLANES
                tap = rt_sc[bi, off:off + t * SUBLANES, :]
                ut = ut + cw_ref[k][None] * tap.reshape(t, SUBLANES, LANES)
            ut_sc[bi] = ut.reshape(t * SUBLANES, LANES)
            u = jnp.concatenate([_from_time_major(ut_sc.at[bi], t, j) for j in range(ncol)], axis=-1)
            uo_ref[bi] = u
        else:
            u = u_ref[bi]
        ub = u.astype(BF16)

        for h in range(heads):
            ch = slice(h * hd, (h + 1) * hd)
            gates = _dot(ub[:, ch], wg_ref[h])
            tr = jnp.tanh(gates[:, :hd] + bg_ref[0:1, ch])
            ti = jnp.tanh(gates[:, hd:] + bg_ref[1:2, ch])
            a = jnp.exp2(half_clam[:, ch] + half_clam[:, ch] * tr)
            m = jnp.maximum((1.0 - a) * (1.0 + a), 0.0)
            bx = (m * lax.rsqrt(jnp.maximum(m, SQRT_GUARD))) * ((0.5 + 0.5 * ti) * u[:, ch])
            for jj in range(hd // LANES):
                j = h * (hd // LANES) + jj
                lanes = slice(jj * LANES, (jj + 1) * LANES)
                at_sc.at[bi][pl.ds(j, t, stride=SUBLANES), :] = a[:, lanes]
                bt_sc.at[bi][pl.ds(j, t, stride=SUBLANES), :] = bx[:, lanes]

    grp = math.gcd(t, SCAN_UNROLL)

    def steps(gi, hs):
        gpos = (t // grp - 1 - gi) if reverse else gi
        base = pl.multiple_of(gpos * grp * SUBLANES, grp * SUBLANES)
        hs = list(hs)
        for k in range(grp):
            rows = pl.ds(base + ((grp - 1 - k) if reverse else k) * SUBLANES, SUBLANES)
            for bi in range(nbat):
                hs[bi] = at_sc[bi, rows, :] * hs[bi] + bt_sc[bi, rows, :]
                ht_sc[bi, rows, :] = hs[bi]
        return tuple(hs)

    hfin = lax.fori_loop(0, t // grp, steps, tuple(carry[bi] for bi in range(nbat)))
    for bi in range(nbat):
        carry[bi] = hfin[bi]
        hfin_ref[bi] = hfin[bi]
        for j in range(ncol):
            lanes = slice(j * LANES, (j + 1) * LANES)
            rec = _from_time_major(ht_sc.at[bi], t, j)
            if gated:
                o_ref[bi, :, lanes] = (g_ref[bi, :, lanes].astype(F32)
                                       * (other_ref[bi, :, lanes] + rec)).astype(o_ref.dtype)
            else:
                o_ref[bi, :, lanes] = rec


def _rglru(src, layer, direction, params, h0, other=None, gate=None):
    conv_w, conv_b, w_gate, b_gate, lam = params
    reverse = bool(direction)
    conv = not reverse
    gated = other is not None
    heads = w_gate.shape[2]
    b, l, w = src.shape
    assert w == SUBLANES * LANES, "time-major layout packs the recurrent width into one (8, 128) tile"
    nbat = max(k for k in (1, 2, SCAN_SEQS) if b % k == 0)
    t = _tile(l, SCAN_CHUNK)
    n = l // t
    tb = t // SUBLANES
    nb = l // SUBLANES

    def pos(i):
        return (n - 1 - i) if reverse else i

    main = pl.BlockSpec((nbat, t, w), lambda bi, i: (bi, pos(i), 0))
    state = pl.BlockSpec((nbat, SUBLANES, LANES), lambda bi, i: (bi, 0, 0))
    in_specs, args = [main], [src]
    if conv:
        in_specs += [
            pl.BlockSpec((nbat, SUBLANES, w), lambda bi, i: (bi, jnp.maximum(pos(i) * tb - 1, 0), 0)),
            pl.BlockSpec((nbat, SUBLANES, w), lambda bi, i: (bi, jnp.minimum((pos(i) + 1) * tb, nb - 1), 0)),
            _layer_block(conv_w, layer), _layer_block(conv_b, layer)]
        args += [src, src, conv_w, conv_b]
    in_specs += [_layer_block(w_gate, layer, direction), _layer_block(b_gate, layer, direction),
                 _layer_block(lam, layer, direction), state]
    args += [w_gate, b_gate, lam, h0]
    if gated:
        in_specs += [main, main]
        args += [other, gate]
    out_specs = [main] + ([main] if conv else []) + [state]
    out_shape = ([jax.ShapeDtypeStruct((b, l, w), BF16 if gated else F32)]
                 + ([jax.ShapeDtypeStruct((b, l, w), F32)] if conv else [])
                 + [jax.ShapeDtypeStruct((b, SUBLANES, LANES), F32)])
    tmaj = pltpu.VMEM((nbat, t * SUBLANES, LANES), F32)
    halo = pltpu.VMEM((nbat, (t + 2 * SUBLANES) * SUBLANES, LANES), F32)
    scratch = ([halo, tmaj] if conv else []) + [tmaj, tmaj, tmaj, pltpu.VMEM((nbat, SUBLANES, LANES), F32)]
    return pl.pallas_call(
        functools.partial(_rglru_kernel, reverse=reverse, conv=conv, gated=gated, heads=heads),
        grid=(b // nbat, n),
        in_specs=in_specs,
        out_specs=out_specs,
        out_shape=out_shape,
        scratch_shapes=scratch,
        compiler_params=_params("arbitrary", "arbitrary"),
        name="rglru_bwd" if reverse else "rglru_fwd",
    )(*args)


def _outproj_kernel(four_ref, rec_ref, w_ref, x_ref, mod_ref, gain_ref, xo_ref, h_ref, y_sc, *, nt):
    i = pl.program_id(1)
    tm = x_ref.shape[0]
    sub = min(tm, OUT_SUB_ROWS)
    post = gain_ref[1] * _mod(mod_ref, 2)
    pre = gain_ref[2] * (1.0 + _mod(mod_ref, 4))
    shift = _mod(mod_ref, 3)
    def step(slot, matmul, epilogue):
        zero = None
        for s in range(tm // sub):
            rows = slice(s * sub, (s + 1) * sub)
            if matmul:
                lhs = jnp.concatenate([four_ref[rows, :], rec_ref[rows, :]], axis=-1)
                if zero is not None:
                    lhs = lhs + zero
                y_sc[slot, rows, :] = _dot(lhs, w_ref[...])
            if epilogue:
                xn = x_ref[rows, :] + _rms(y_sc[1 - slot, rows, :], post)
                xo_ref[rows, :] = xn
                hv = _rms(xn, pre) + shift
                h_ref[rows, :] = hv.astype(h_ref.dtype)
                if matmul:
                    acc = hv.reshape(sub // SUBLANES, SUBLANES, hv.shape[-1]).sum(axis=0)
                    bits = pltpu.bitcast(acc, jnp.uint32)
                    zero = ((bits >> 16) >> 16).astype(F32).astype(BF16)
                    zero = jnp.concatenate([zero, zero], axis=0).reshape(1, 2 * SUBLANES, -1)
                    zero = jnp.broadcast_to(zero, (sub // (2 * SUBLANES), 2 * SUBLANES, zero.shape[-1])
                                            ).reshape(sub, -1)

    pl.when(i == 0)(functools.partial(step, 0, True, False))
    for parity in (0, 1):
        pl.when((i > 0) & (i < nt) & (i % 2 == parity))(functools.partial(step, parity, True, True))
    pl.when(i == nt)(functools.partial(step, nt % 2, False, True))


def _outproj(four, rec, w_out, x, mods, row_of, gains, layer):
    b, l, d = x.shape
    fw, lw = four.shape[-1], rec.shape[-1]
    tm = _tile(l, ROW_TILE)
    nt = l // tm
    ahead = lambda bi, i: (bi, jnp.minimum(i, nt - 1), 0)
    behind = lambda bi, i: (bi, jnp.maximum(i - 1, 0), 0)
    return pl.pallas_call(
        functools.partial(_outproj_kernel, nt=nt),
        grid=(b, nt + 1),
        in_specs=[pl.BlockSpec((None, tm, fw), ahead), pl.BlockSpec((None, tm, lw), ahead),
                  _layer_block(w_out, layer), pl.BlockSpec((None, tm, d), behind),
                  _mod_spec(mods, layer, row_of), _layer_block(gains, layer)],
        out_specs=[pl.BlockSpec((None, tm, d), behind), pl.BlockSpec((None, tm, d), behind)],
        out_shape=[jax.ShapeDtypeStruct((b, l, d), F32), jax.ShapeDtypeStruct((b, l, d), BF16)],
        scratch_shapes=[pltpu.VMEM((2, tm, d), F32)],
        compiler_params=_params("arbitrary", "arbitrary"),
        name="out_proj",
    )(four, rec, w_out, x, mods, gains)


def _ffn_kernel(h_ref, wg_ref, wu_ref, wo_ref, x_ref, mod_ref, gain_ref, o_ref, *, nj):
    j = pl.program_id(2)
    tm, d = o_ref.shape
    sub = min(tm, FFN_SUB_ROWS)
    post = gain_ref[3] * _mod(mod_ref, 5)

    def chunk(first, last):
        for s in range(tm // sub):
            rows = slice(s * sub, (s + 1) * sub)
            h = h_ref[rows, :]
            g = _dot(h, wg_ref[...])
            u = _dot(h, wu_ref[...])
            act = (g * jax.nn.sigmoid(g) * u).astype(BF16)
            for c in range(0, d, FFN_OUT_COLS):
                cols = slice(c, c + FFN_OUT_COLS)
                part = _dot(act, wo_ref[:, cols])
                o_ref[rows, cols] = part if first else o_ref[rows, cols] + part
            if last:
                o_ref[rows, :] = x_ref[rows, :] + _rms(o_ref[rows, :], post)

    if nj == 1:
        chunk(True, True)
    else:
        pl.when(j == 0)(functools.partial(chunk, True, False))
        pl.when((j > 0) & (j < nj - 1))(functools.partial(chunk, False, False))
        pl.when(j == nj - 1)(functools.partial(chunk, False, True))


def _ffn(h, w_ffn_in, w_ffn_out, x, mods, row_of, gains, layer):
    b, l, d = x.shape
    dff = w_ffn_out.shape[1]
    tm = _tile(l, FFN_ROW_TILE)
    tf = _tile(dff, FFN_COL_TILE)
    nj = dff // tf
    row = lambda bi, i, j: (bi, i, 0)
    return pl.pallas_call(
        functools.partial(_ffn_kernel, nj=nj),
        grid=(b, l // tm, nj),
        in_specs=[pl.BlockSpec((None, tm, d), row),
                  pl.BlockSpec((None, d, tf), lambda bi, i, j: (layer, 0, j)),
                  pl.BlockSpec((None, d, tf), lambda bi, i, j: (layer, 0, nj + j)),
                  pl.BlockSpec((None, tf, d), lambda bi, i, j: (layer, j, 0)),
                  pl.BlockSpec((None, tm, d), row),
                  _mod_spec(mods, layer, row_of), _layer_block(gains, layer)],
        out_specs=pl.BlockSpec((None, tm, d), row),
        out_shape=jax.ShapeDtypeStruct((b, l, d), F32),
        compiler_params=_params("arbitrary", "arbitrary", "arbitrary", vmem=60 * 1024 * 1024),
        name="ffn",
    )(h, w_ffn_in, w_ffn_in, w_ffn_out, x, mods, gains)


def kernel(x, c, ctx, c_ctx, w_ada, b_ada, norm_g, w_in, w_four, conv_w, conv_b, lru_w, lru_b, lru_lam,
           w_out, w_ffn_in, w_ffn_out):
    depth = w_ada.shape[0]
    bsz, _, d = x.shape
    lc = ctx.shape[1]
    fw = w_four.shape[1] * w_four.shape[2]
    lw = conv_w.shape[-1]

    pad = (-(bsz + 1)) % SUBLANES
    cs = jnp.concatenate([c, c_ctx[None, :], jnp.zeros((pad, d), F32)], axis=0)
    mods = _ada(cs, w_ada, b_ada)
    mods = mods.reshape(depth, mods.shape[1], 1, N_MOD * d)
    lat_row = lambda bi: bi
    ctx_row = lambda bi: bsz

    gains = norm_g.reshape(depth, norm_g.shape[1], 1, d)
    w_in_b = w_in.astype(BF16)
    w_out_b = w_out.astype(BF16)
    w_ffn_in_b = w_ffn_in.astype(BF16)
    w_ffn_out_b = w_ffn_out.astype(BF16)
    w_four_b = w_four.astype(BF16)
    w_gate_b = (0.5 * jnp.concatenate([lru_w[:, :, 0], lru_w[:, :, 1]], axis=-1)).astype(BF16)
    scan_params = (conv_w.reshape(depth, conv_w.shape[1], SUBLANES, LANES),
                   conv_b.reshape(depth, SUBLANES, LANES), w_gate_b, 0.5 * lru_b,
                   lru_lam.reshape(depth, 2, 1, lw))
    zero_state = jnp.zeros((bsz, SUBLANES, LANES), F32)

    ctx = ctx.reshape(1, bsz * lc, d)
    per_seq = lambda a: a.reshape(bsz, -1, a.shape[-1])
    flat = lambda a: a.reshape(1, bsz * lc, a.shape[-1])

    for layer in range(depth):
        last = layer == depth - 1
        f_lat, r_lat, g_lat = _inproj(x, mods, lat_row, gains, w_in_b, layer, fw, lw)
        f_ctx, r_ctx, g_ctx = _inproj(ctx, mods, ctx_row, gains, w_in_b, layer, fw, lw)

        recf_ctx, u_ctx, hf = _rglru(per_seq(r_ctx), layer, 0, scan_params, zero_state)
        if last:
            _, hb = _rglru(u_ctx, layer, 1, scan_params, zero_state)
        else:
            rec_ctx, hb = _rglru(u_ctx, layer, 1, scan_params, zero_state, recf_ctx, per_seq(g_ctx))
        recf_lat, u_lat, _ = _rglru(r_lat, layer, 0, scan_params, hf)
        rec_lat, _ = _rglru(u_lat, layer, 1, scan_params, hb, recf_lat, g_lat)

        four_lat = _fourier(f_lat, w_four_b, layer, GRID_W)
        x, h_lat = _outproj(four_lat, rec_lat, w_out_b, x, mods, lat_row, gains, layer)
        x = _ffn(h_lat, w_ffn_in_b, w_ffn_out_b, x, mods, lat_row, gains, layer)
        if not last:
            four_ctx = _fourier(per_seq(f_ctx), w_four_b, layer, None)
            ctx, h_ctx = _outproj(flat(four_ctx), flat(rec_ctx), w_out_b, ctx, mods, ctx_row, gains, layer)
            ctx = _ffn(h_ctx, w_ffn_in_b, w_ffn_out_b, ctx, mods, ctx_row, gains, layer)
    return x
```
